```python
import jax, jax.numpy as jnp
from jax import lax
import numpy as np

D_MODEL = 1024
BATCH = 8
SEQ = 4096
DEPTH = 4

CHUNK = 64
PLE_DIM = 256
EPS = 1e-6
A_HEADS = 4
A_DK = 128
A_DV = 128
A_WIDTH = A_HEADS * A_DV
B_HEADS = 8
B_HD = 64
B_WIDTH = B_HEADS * B_HD
B_LEFT_CHUNKS = 8
B_BAND = (B_LEFT_CHUNKS + 1) * CHUNK
MAX_REL = 128
N_REL = 2 * MAX_REL + 1
D_FF = ((8 * D_MODEL // 3 + 255) // 256) * 256
IN_SIZES = (A_HEADS * A_DK, A_HEADS * A_DK, A_WIDTH, A_WIDTH,
            B_WIDTH, B_WIDTH, B_WIDTH,
            D_MODEL, D_MODEL)
D_IN = sum(IN_SIZES)

kernel_name = "hybrid_hgrn2_bandattn_streaming_trunk"


def rmsnorm(x, g):
    x32 = x.astype(jnp.float32)
    y = x32 * lax.rsqrt(jnp.mean(x32 * x32, axis=-1, keepdims=True) + EPS)
    return (y * g.astype(jnp.float32)).astype(x.dtype)


def head_rms(t, g):
    t32 = t.astype(jnp.float32)
    return t32 * lax.rsqrt(jnp.mean(t32 * t32, axis=-1, keepdims=True) + EPS) * g.astype(jnp.float32)


def hgrn2_mixer(q, f_logit, i, g, lb, onorm_g):
    f32 = jnp.float32
    bsz, seq, _ = q.shape
    nc = seq // CHUNK
    lb32 = lb.astype(f32)
    logf = jnp.logaddexp(jnp.log(lb32), jnp.log1p(-lb32) + jax.nn.log_sigmoid(f_logit.astype(f32)))
    k = 1.0 - jnp.exp(logf)

    def heads(t, d):
        return t.astype(f32).reshape(bsz, nc, CHUNK, A_HEADS, d).transpose(1, 0, 3, 2, 4)

    qc = heads(q, A_DK) * (A_DK ** -0.5)
    kc = heads(k, A_DK)
    lfc = heads(logf, A_DK)
    ic = heads(i, A_DV)
    causal = jnp.tril(jnp.ones((CHUNK, CHUNK), dtype=bool))

    def step(state, inp):
        qt, kt, lft, it = inp
        b = jnp.cumsum(lft, axis=-2)
        o_inter = jnp.einsum('bhtk,bhkv->bhtv', qt * jnp.exp(b), state)
        diff = b[..., :, None, :] - b[..., None, :, :]
        dec = jnp.exp(jnp.where(causal[:, :, None], diff, -jnp.inf))
        attn = jnp.einsum('bhtk,bhsk,bhtsk->bhts', qt, kt, dec)
        o = o_inter + jnp.einsum('bhts,bhsv->bhtv', attn, it)
        b_last = b[..., -1:, :]
        new_state = (jnp.exp(b_last[..., 0, :])[..., None] * state
                     + jnp.einsum('bhsk,bhsv->bhkv', kt * jnp.exp(b_last - b), it))
        return new_state, o

    s0 = jnp.zeros((bsz, A_HEADS, A_DK, A_DV), f32)
    _, o = lax.scan(step, s0, (qc, kc, lfc, ic))
    o = o.transpose(1, 0, 3, 2, 4).reshape(bsz, seq, A_HEADS, A_DV)
    o = head_rms(o, onorm_g).reshape(bsz, seq, A_WIDTH) * jax.nn.silu(g.astype(f32))
    return o.astype(q.dtype)


def band_attention(q, k, v, qn_g, kn_g, rel_bias):
    f32 = jnp.float32
    bsz, seq, _ = q.shape
    nc = seq // CHUNK
    qh = head_rms(q.reshape(bsz, seq, B_HEADS, B_HD), qn_g)
    kh = head_rms(k.reshape(bsz, seq, B_HEADS, B_HD), kn_g)
    vh = v.reshape(bsz, seq, B_HEADS, B_HD).astype(f32)
    pad = B_LEFT_CHUNKS * CHUNK
    kp = jnp.pad(kh, ((0, 0), (pad, 0), (0, 0), (0, 0)))
    vp = jnp.pad(vh, ((0, 0), (pad, 0), (0, 0), (0, 0)))
    band_idx = (jnp.arange(nc) * CHUNK)[:, None] + jnp.arange(B_BAND)[None, :]
    kb = kp[:, band_idx]
    vb = vp[:, band_idx]
    qc = qh.reshape(bsz, nc, CHUNK, B_HEADS, B_HD)
    scores = jnp.einsum('bnqhd,bnkhd->bnhqk', qc, kb) * (B_HD ** -0.5)
    rel = (pad + jnp.arange(CHUNK))[:, None] - jnp.arange(B_BAND)[None, :]
    bias = rel_bias.astype(f32)[:, jnp.clip(rel, -MAX_REL, MAX_REL) + MAX_REL]
    valid = band_idx >= pad
    scores = jnp.where(valid[None, :, None, None, :], scores + bias[None, None], -1e30)
    probs = jax.nn.softmax(scores, axis=-1)
    o = jnp.einsum('bnhqk,bnkhd->bnqhd', probs, vb)
    return o.reshape(bsz, seq, B_WIDTH).astype(q.dtype)


def setup_inputs(seed: int = 0) -> dict:
    key = jax.random.key(seed)
    ks = jax.random.split(key, 24)
    f32 = jnp.float32

    def nrm(k, shape, scale):
        return jax.random.normal(k, shape, f32) * scale

    def gain(k, shape):
        return 1.0 + 0.02 * jax.random.normal(k, shape, f32)

    return {
        "x": nrm(ks[0], (BATCH, SEQ, D_MODEL), 1.0),
        "p": nrm(ks[1], (DEPTH, BATCH, SEQ, PLE_DIM), 1.0),
        "norm_mix_g": gain(ks[2], (DEPTH, D_MODEL)),
        "w_in": nrm(ks[3], (DEPTH, D_MODEL, D_IN), D_MODEL ** -0.5),
        "hgrn_lb_logits": nrm(ks[4], (DEPTH, A_HEADS * A_DK), 0.5),
        "hgrn_onorm_g": gain(ks[5], (DEPTH, A_DV)),
        "attn_qnorm_g": gain(ks[6], (DEPTH, B_HD)),
        "attn_knorm_g": gain(ks[7], (DEPTH, B_HD)),
        "attn_rel_bias": nrm(ks[8], (DEPTH, B_HEADS, N_REL), 0.3),
        "w_branch_a": nrm(ks[9], (DEPTH, A_WIDTH, D_MODEL), A_WIDTH ** -0.5),
        "w_branch_b": nrm(ks[10], (DEPTH, B_WIDTH, D_MODEL), B_WIDTH ** -0.5),
        "w_out": nrm(ks[11], (DEPTH, D_MODEL, D_MODEL), 0.5 * D_MODEL ** -0.5),
        "norm_ffn_g": gain(ks[12], (DEPTH, D_MODEL)),
        "w_ffn_gate": nrm(ks[13], (DEPTH, D_MODEL, D_FF), D_MODEL ** -0.5),
        "w_ffn_up": nrm(ks[14], (DEPTH, D_MODEL, D_FF), D_MODEL ** -0.5),
        "w_ffn_down": nrm(ks[15], (DEPTH, D_FF, D_MODEL), 0.5 * D_FF ** -0.5),
        "norm_ple_g": gain(ks[16], (DEPTH, D_MODEL)),
        "w_ple_gate": nrm(ks[17], (DEPTH, D_MODEL, D_MODEL), D_MODEL ** -0.5),
        "w_ple_proj": nrm(ks[18], (DEPTH, PLE_DIM, D_MODEL), 0.5 * PLE_DIM ** -0.5),
    }


def reference(x, p, norm_mix_g, w_in, hgrn_lb_logits, hgrn_onorm_g, attn_qnorm_g, attn_knorm_g,
              attn_rel_bias, w_branch_a, w_branch_b, w_out, norm_ffn_g, w_ffn_gate, w_ffn_up,
              w_ffn_down, norm_ple_g, w_ple_gate, w_ple_proj):
    lb_soft = jax.nn.softmax(hgrn_lb_logits.astype(jnp.float32), axis=0)
    lb_cum = jnp.cumsum(lb_soft, axis=0)
    lower_bounds = lb_cum - lb_cum[0:1]
    split_at = [int(s) for s in np.cumsum(IN_SIZES)[:-1]]

    for li in range(DEPTH):
        h = rmsnorm(x, norm_mix_g[li])
        z = h @ w_in[li]
        qa, fa, ia, ga, qb, kb, vb, gate_a, gate_b = jnp.split(z, split_at, axis=-1)
        oa = hgrn2_mixer(qa, fa, ia, ga, lower_bounds[li], hgrn_onorm_g[li])
        ob = band_attention(qb, kb, vb, attn_qnorm_g[li], attn_knorm_g[li], attn_rel_bias[li])
        merged = (jax.nn.sigmoid(gate_a) * (oa @ w_branch_a[li])
                  + jax.nn.sigmoid(gate_b) * (ob @ w_branch_b[li]))
        x = x + merged @ w_out[li]
        h2 = rmsnorm(x, norm_ffn_g[li])
        x = x + (jax.nn.silu(h2 @ w_ffn_gate[li]) * (h2 @ w_ffn_up[li])) @ w_ffn_down[li]
        h3 = rmsnorm(x, norm_ple_g[li])
        x = x + jax.nn.sigmoid(h3 @ w_ple_gate[li]) * (p[li].astype(x.dtype) @ w_ple_proj[li])
    return x
```

```python
import functools

import jax
import jax.numpy as jnp
from jax import lax
from jax.experimental import pallas as pl
from jax.experimental.pallas import tpu as pltpu

F32 = jnp.float32
BF16 = jnp.bfloat16

D_MODEL = 1024
BATCH = 8
SEQ = 4096
DEPTH = 4
TOKENS = BATCH * SEQ
CHUNK = 64
PLE_DIM = 256
EPS = 1e-6
A_HEADS = 4
A_DK = 128
A_DV = 128
A_WIDTH = A_HEADS * A_DV
B_HEADS = 8
B_HD = 64
B_WIDTH = B_HEADS * B_HD
B_LEFT_CHUNKS = 8
B_PAD = B_LEFT_CHUNKS * CHUNK
B_BAND = B_PAD + CHUNK
MAX_REL = 128
D_FF = 2816
D_IN = 5632
NEG_BIG = -1e30

COL_W = 512
COL_QA, COL_FA, COL_IA, COL_GA, COL_QB, COL_KB, COL_VB = 0, 1, 2, 3, 4, 5, 6
COL_GATE_A = 7
COL_GATE_B = 9

SUBLANES = 8
LANES = 128
VMEM_LIMIT_BYTES = 56 * 1024 * 1024

INPROJ_TM = 1024
HGRN_ROWS = 512
ATTN_ROWS = B_PAD
MERGE_TM = 512
FFN_TM = 512
PLE_TM = 512
SUB = SUBLANES
N_SUB = CHUNK // SUB


def _params(*sem):
    return pltpu.CompilerParams(dimension_semantics=sem, vmem_limit_bytes=VMEM_LIMIT_BYTES)


def _resident(shape):
    nd = len(shape)
    return pl.BlockSpec(shape, lambda *_: (0,) * nd, pipeline_mode=pl.Buffered(1))


def _rmsnorm_bf16(x, g):
    ms = jnp.mean(x * x, axis=-1, keepdims=True)
    return (x * lax.rsqrt(ms + EPS) * g).astype(BF16)


def _sigmoid(x):
    return 1.0 / (1.0 + jnp.exp(-x))


def _log_sigmoid(x):
    return jnp.minimum(x, 0.0) - jnp.log1p(jnp.exp(-jnp.abs(x)))


def _inproj_kernel(li, x_ref, g_ref, w_ref, lbl_ref, z_ref, logf_ref, h_ref):
    j = pl.program_id(1)

    @pl.when(j == 0)
    def _():
        h_ref[...] = _rmsnorm_bf16(x_ref[...], g_ref[...])

    acc = jnp.dot(h_ref[...], w_ref[...], preferred_element_type=F32)
    z_ref[...] = acc.astype(BF16)

    @pl.when(j == COL_FA)
    def _():
        lbl = lbl_ref[...]
        e = jnp.exp(lbl - jnp.max(lbl, axis=0, keepdims=True))
        soft = e / jnp.sum(e, axis=0, keepdims=True)
        lb = jnp.zeros((1, COL_W), F32)
        for r in range(1, li + 1):
            lb = lb + soft[r:r + 1, :]
        log_lb = jnp.log(lb)
        c = jnp.log1p(-lb) + _log_sigmoid(acc)
        mx = jnp.maximum(log_lb, c)
        logf_ref[...] = mx + jnp.log1p(jnp.exp(-jnp.abs(log_lb - c)))


def _inproj(li, x, g, w, lb_logits):
    tm = INPROJ_TM
    return pl.pallas_call(
        functools.partial(_inproj_kernel, li),
        grid=(TOKENS // tm, D_IN // COL_W),
        in_specs=[
            pl.BlockSpec((tm, D_MODEL), lambda i, j: (i, 0)),
            pl.BlockSpec((1, D_MODEL), lambda i, j: (0, 0)),
            pl.BlockSpec((D_MODEL, COL_W), lambda i, j: (0, j)),
            pl.BlockSpec((DEPTH, COL_W), lambda i, j: (0, 0)),
        ],
        out_specs=[
            pl.BlockSpec((tm, COL_W), lambda i, j: (i, j)),
            pl.BlockSpec((tm, COL_W), lambda i, j: (i, 0)),
        ],
        out_shape=[
            jax.ShapeDtypeStruct((TOKENS, D_IN), BF16),
            jax.ShapeDtypeStruct((TOKENS, COL_W), F32),
        ],
        scratch_shapes=[pltpu.VMEM((tm, D_MODEL), BF16)],
        compiler_params=_params("arbitrary", "arbitrary"),
        name="inproj",
    )(x, g, w, lb_logits)


def _split3_bf16(v):
    hi = v.astype(BF16)
    r1 = v - hi.astype(F32)
    mid = r1.astype(BF16)
    lo = (r1 - mid.astype(F32)).astype(BF16)
    return hi, mid, lo


def _hgrn_head_chunk(qs, lf, iv, st):
    row = lax.broadcasted_iota(jnp.int32, (CHUNK, CHUNK), 0)
    col = lax.broadcasted_iota(jnp.int32, (CHUNK, CHUNK), 1)
    tril = jnp.where(row >= col, 1.0, 0.0).astype(BF16)
    hi, mid, lo = _split3_bf16(lf)
    b = (jnp.dot(tril, hi, preferred_element_type=F32)
         + jnp.dot(tril, mid, preferred_element_type=F32)
         + jnp.dot(tril, lo, preferred_element_type=F32))
    k = 1.0 - jnp.exp(lf)
    b_last = b[CHUNK - 1:CHUNK, :]

    o = lax.dot_general((qs * jnp.exp(b)).astype(BF16), st.astype(BF16),
                        (((1,), (1,)), ((), ())), preferred_element_type=F32)

    zero_blk = jnp.zeros((SUB, A_DK), F32)
    c_rows = [None] + [b[SUB * i - 1:SUB * i, :] for i in range(1, N_SUB)]
    cref = jnp.concatenate(
        [zero_blk] + [jnp.broadcast_to(c_rows[i], (SUB, A_DK)) for i in range(1, N_SUB)], axis=0)
    q_all = qs * jnp.exp(b - cref)
    q_parts, k_parts = [], []
    for i in range(1, N_SUB):
        q_parts.append(jnp.concatenate(
            [zero_blk] * i + [q_all[SUB * i:SUB * (i + 1), :]] + [zero_blk] * (N_SUB - 1 - i), axis=0))
        kd = k[:SUB * i, :] * jnp.exp(c_rows[i] - b[:SUB * i, :])
        k_parts.append(jnp.concatenate([kd, jnp.zeros((CHUNK - SUB * i, A_DK), F32)], axis=0))
    q_cat = jnp.concatenate(q_parts, axis=1).astype(BF16)
    k_cat = jnp.concatenate(k_parts, axis=1).astype(BF16)
    attn = lax.dot_general(q_cat, k_cat, (((1,), (1,)), ((), ())),
                           preferred_element_type=F32)

    t_loc = lax.broadcasted_iota(jnp.int32, (SUB, CHUNK), 0)
    lane = lax.broadcasted_iota(jnp.int32, (SUB, CHUNK), 1)
    diag_blocks = []
    for i in range(N_SUB):
        qi = qs[SUB * i:SUB * (i + 1), :]
        bi = b[SUB * i:SUB * (i + 1), :]
        blk = jnp.zeros((SUB, CHUNK), F32)
        for jj in range(SUB):
            s = SUB * i + jj
            p = qi * k[s:s + 1, :] * jnp.exp(bi - b[s:s + 1, :])
            colsum = jnp.sum(p, axis=-1, keepdims=True)
            blk = jnp.where((lane == s) & (t_loc >= jj), colsum, blk)
        diag_blocks.append(blk)
    attn = attn + jnp.concatenate(diag_blocks, axis=0)

    o = o + jnp.dot(attn.astype(BF16), iv, preferred_element_type=F32)

    k_dec = (k * jnp.exp(b_last - b)).astype(BF16)
    st_new = st * jnp.exp(b_last) + lax.dot_general(
        iv, k_dec, (((0,), (0,)), ((), ())), preferred_element_type=F32)
    return o, st_new


def _hgrn_kernel(q_ref, lf_ref, i_ref, g_ref, og_ref, out_ref, st_ref):
    @pl.when(pl.program_id(1) == 0)
    def _():
        st_ref[...] = jnp.zeros_like(st_ref)

    def chunk_body(c, carry):
        r0 = pl.multiple_of(c * CHUNK, CHUNK)
        rows = pl.ds(r0, CHUNK)
        for h in range(A_HEADS):
            cols = slice(h * A_DK, (h + 1) * A_DK)
            qs = q_ref[rows, cols].astype(F32) * (A_DK ** -0.5)
            o, st_new = _hgrn_head_chunk(qs, lf_ref[rows, cols], i_ref[rows, cols], st_ref[h])
            st_ref[h] = st_new
            ms = jnp.mean(o * o, axis=-1, keepdims=True)
            g = g_ref[rows, cols].astype(F32)
            y = o * lax.rsqrt(ms + EPS) * og_ref[...] * (g * _sigmoid(g))
            out_ref[rows, cols] = y.astype(BF16)
        return carry

    lax.fori_loop(0, HGRN_ROWS // CHUNK, chunk_body, 0)


def _hgrn(z, logf, onorm_g):
    nb = SEQ // HGRN_ROWS

    def zcol(cb):
        return pl.BlockSpec((HGRN_ROWS, COL_W), lambda b, n: (b * nb + n, cb))

    return pl.pallas_call(
        _hgrn_kernel,
        grid=(BATCH, nb),
        in_specs=[
            zcol(COL_QA),
            pl.BlockSpec((HGRN_ROWS, COL_W), lambda b, n: (b * nb + n, 0)),
            zcol(COL_IA),
            zcol(COL_GA),
            pl.BlockSpec((1, A_DV), lambda b, n: (0, 0)),
        ],
        out_specs=pl.BlockSpec((HGRN_ROWS, A_WIDTH), lambda b, n: (b * nb + n, 0)),
        out_shape=jax.ShapeDtypeStruct((TOKENS, A_WIDTH), BF16),
        scratch_shapes=[pltpu.VMEM((A_HEADS, A_DV, A_DK), F32)],
        compiler_params=_params("arbitrary", "arbitrary"),
        name="hgrn2",
    )(z, logf, z, z, onorm_g)


def _pair_head_rms(t, g_pair):
    lane = lax.broadcasted_iota(jnp.int32, t.shape, 1)
    first = lane < B_HD
    sq = t * t
    s0 = jnp.sum(jnp.where(first, sq, 0.0), axis=-1, keepdims=True)
    s1 = jnp.sum(jnp.where(first, 0.0, sq), axis=-1, keepdims=True)
    ms = jnp.where(first, s0, s1) * (1.0 / B_HD)
    return t * lax.rsqrt(ms + EPS) * g_pair


def _attn_kernel(q_ref, kp_ref, kc_ref, vp_ref, vc_ref, qg_ref, kg_ref, bias_ref,
                 out_ref, kw_ref, vw_ref, qn_ref):
    n = pl.program_id(1)
    rows = ATTN_ROWS
    for p in range(B_HEADS // 2):
        cols = slice(p * LANES, (p + 1) * LANES)
        kw_ref[0:rows, cols] = _pair_head_rms(kp_ref[:, cols].astype(F32), kg_ref[...]).astype(BF16)
        kw_ref[rows:2 * rows, cols] = _pair_head_rms(kc_ref[:, cols].astype(F32), kg_ref[...]).astype(BF16)
        qn_ref[:, cols] = _pair_head_rms(q_ref[:, cols].astype(F32), qg_ref[...]) * (B_HD ** -0.5)
    vw_ref[0:rows, :] = vp_ref[...]
    vw_ref[rows:2 * rows, :] = vc_ref[...]

    lane = lax.broadcasted_iota(jnp.int32, (CHUNK, LANES), 1)
    key_pos = lax.broadcasted_iota(jnp.int32, (CHUNK, B_BAND), 1)

    def chunk_body(c, carry):
        q0 = pl.multiple_of(c * CHUNK, CHUNK)
        qrows = pl.ds(q0, CHUNK)
        krows = pl.ds(q0, B_BAND)
        valid = (n > 0) | (key_pos + q0 >= rows)
        for p in range(B_HEADS // 2):
            cols = slice(p * LANES, (p + 1) * LANES)
            qp = qn_ref[qrows, cols]
            kwin = kw_ref[krows, cols]
            vwin = vw_ref[krows, cols]
            o_pair = jnp.zeros((CHUNK, LANES), F32)
            for hh in range(2):
                head_lanes = (lane < B_HD) if hh == 0 else (lane >= B_HD)
                qh = jnp.where(head_lanes, qp, 0.0).astype(BF16)
                s = lax.dot_general(qh, kwin, (((1,), (1,)), ((), ())),
                                    preferred_element_type=F32)
                s = jnp.where(valid, s + bias_ref[2 * p + hh], NEG_BIG)
                m = jnp.max(s, axis=-1, keepdims=True)
                e = jnp.exp(s - m)
                l = jnp.sum(e, axis=-1, keepdims=True)
                o = jnp.dot(e.astype(BF16), vwin, preferred_element_type=F32) / l
                o_pair = jnp.where(head_lanes, o, o_pair)
            out_ref[qrows, cols] = o_pair.astype(BF16)
        return carry

    lax.fori_loop(0, rows // CHUNK, chunk_body, 0)


def _band_bias(rel_bias):
    rel = (B_PAD + jnp.arange(CHUNK))[:, None] - jnp.arange(B_BAND)[None, :]
    return rel_bias.astype(F32)[:, jnp.clip(rel, -MAX_REL, MAX_REL) + MAX_REL]


def _attn(z, qn_g, kn_g, bias):
    rows = ATTN_ROWS
    nb = SEQ // rows

    def cur(cb):
        return pl.BlockSpec((rows, COL_W), lambda b, n: (b * nb + n, cb))

    def prev(cb):
        return pl.BlockSpec((rows, COL_W), lambda b, n: (b * nb + jnp.maximum(n - 1, 0), cb))

    qg2 = jnp.concatenate([qn_g, qn_g]).reshape(1, LANES)
    kg2 = jnp.concatenate([kn_g, kn_g]).reshape(1, LANES)
    return pl.pallas_call(
        _attn_kernel,
        grid=(BATCH, nb),
        in_specs=[
            cur(COL_QB), prev(COL_KB), cur(COL_KB), prev(COL_VB), cur(COL_VB),
            pl.BlockSpec((1, LANES), lambda b, n: (0, 0)),
            pl.BlockSpec((1, LANES), lambda b, n: (0, 0)),
            pl.BlockSpec((B_HEADS, CHUNK, B_BAND), lambda b, n: (0, 0, 0)),
        ],
        out_specs=pl.BlockSpec((rows, B_WIDTH), lambda b, n: (b * nb + n, 0)),
        out_shape=jax.ShapeDtypeStruct((TOKENS, B_WIDTH), BF16),
        scratch_shapes=[
            pltpu.VMEM((2 * rows, B_WIDTH), BF16),
            pltpu.VMEM((2 * rows, B_WIDTH), BF16),
            pltpu.VMEM((rows, B_WIDTH), F32),
        ],
        compiler_params=_params("arbitrary", "arbitrary"),
        name="band_attn",
    )(z, z, z, z, z, qg2, kg2, bias)


def _merge_kernel(x_ref, oa_ref, ob_ref, ga0_ref, ga1_ref, gb0_ref, gb1_ref,
                  wa_ref, wb_ref, wo_ref, out_ref):
    ga = jnp.concatenate([ga0_ref[...], ga1_ref[...]], axis=1).astype(F32)
    gb = jnp.concatenate([gb0_ref[...], gb1_ref[...]], axis=1).astype(F32)
    pa = jnp.dot(oa_ref[...], wa_ref[...], preferred_element_type=F32)
    pb = jnp.dot(ob_ref[...], wb_ref[...], preferred_element_type=F32)
    merged = (_sigmoid(ga) * pa + _sigmoid(gb) * pb).astype(BF16)
    out_ref[...] = x_ref[...] + jnp.dot(merged, wo_ref[...], preferred_element_type=F32)


def _merge(x, oa, ob, z, wa, wb, wo):
    tm = MERGE_TM

    def zcol(cb):
        return pl.BlockSpec((tm, COL_W), lambda i: (i, cb))

    return pl.pallas_call(
        _merge_kernel,
        grid=(TOKENS // tm,),
        in_specs=[
            pl.BlockSpec((tm, D_MODEL), lambda i: (i, 0)),
            pl.BlockSpec((tm, A_WIDTH), lambda i: (i, 0)),
            pl.BlockSpec((tm, B_WIDTH), lambda i: (i, 0)),
            zcol(COL_GATE_A), zcol(COL_GATE_A + 1), zcol(COL_GATE_B), zcol(COL_GATE_B + 1),
            _resident((A_WIDTH, D_MODEL)),
            _resident((B_WIDTH, D_MODEL)),
            _resident((D_MODEL, D_MODEL)),
        ],
        out_specs=pl.BlockSpec((tm, D_MODEL), lambda i: (i, 0)),
        out_shape=jax.ShapeDtypeStruct((TOKENS, D_MODEL), F32),
        compiler_params=_params("arbitrary"),
        name="merge_out",
    )(x, oa, ob, z, z, z, z, wa, wb, wo)


def _ffn_kernel(x_ref, g_ref, wg_ref, wu_ref, wd_ref, out_ref):
    x = x_ref[...]
    h = _rmsnorm_bf16(x, g_ref[...])
    gate = jnp.dot(h, wg_ref[...], preferred_element_type=F32)
    up = jnp.dot(h, wu_ref[...], preferred_element_type=F32)
    act = (gate * _sigmoid(gate) * up).astype(BF16)
    out_ref[...] = x + jnp.dot(act, wd_ref[...], preferred_element_type=F32)


def _ffn(x, g, wg, wu, wd):
    tm = FFN_TM
    return pl.pallas_call(
        _ffn_kernel,
        grid=(TOKENS // tm,),
        in_specs=[
            pl.BlockSpec((tm, D_MODEL), lambda i: (i, 0)),
            _resident((1, D_MODEL)),
            _resident((D_MODEL, D_FF)),
            _resident((D_MODEL, D_FF)),
            _resident((D_FF, D_MODEL)),
        ],
        out_specs=pl.BlockSpec((tm, D_MODEL), lambda i: (i, 0)),
        out_shape=jax.ShapeDtypeStruct((TOKENS, D_MODEL), F32),
        compiler_params=_params("arbitrary"),
        name="ffn",
    )(x, g, wg, wu, wd)


def _ple_kernel(x_ref, p_ref, g_ref, wg_ref, wp_ref, out_ref):
    x = x_ref[...]
    h = _rmsnorm_bf16(x, g_ref[...])
    gate = _sigmoid(jnp.dot(h, wg_ref[...], preferred_element_type=F32))
    emb = jnp.dot(p_ref[...].astype(BF16), wp_ref[...], preferred_element_type=F32)
    out_ref[...] = x + gate * emb


def _ple(li, x, p, g, wg, wp):
    tm = PLE_TM
    return pl.pallas_call(
        _ple_kernel,
        grid=(TOKENS // tm,),
        in_specs=[
            pl.BlockSpec((tm, D_MODEL), lambda i: (i, 0)),
            pl.BlockSpec((None, tm, PLE_DIM), lambda i: (li, i, 0)),
            _resident((1, D_MODEL)),
            _resident((D_MODEL, D_MODEL)),
            _resident((PLE_DIM, D_MODEL)),
        ],
        out_specs=pl.BlockSpec((tm, D_MODEL), lambda i: (i, 0)),
        out_shape=jax.ShapeDtypeStruct((TOKENS, D_MODEL), F32),
        compiler_params=_params("arbitrary"),
        name="ple",
    )(x, p, g, wg, wp)


def kernel(x, p, norm_mix_g, w_in, hgrn_lb_logits, hgrn_onorm_g, attn_qnorm_g, attn_knorm_g,
           attn_rel_bias, w_branch_a, w_branch_b, w_out, norm_ffn_g, w_ffn_gate, w_ffn_up,
           w_ffn_down, norm_ple_g, w_ple_gate, w_ple_proj):
    xt = x.reshape(TOKENS, D_MODEL)
    pt = p.reshape(DEPTH, TOKENS, PLE_DIM)
    for li in range(DEPTH):
        z, logf = _inproj(li, xt, norm_mix_g[li].reshape(1, D_MODEL), w_in[li].astype(BF16),
                          hgrn_lb_logits)
        oa = _hgrn(z, logf, hgrn_onorm_g[li].reshape(1, A_DV))
        ob = _attn(z, attn_qnorm_g[li], attn_knorm_g[li], _band_bias(attn_rel_bias[li]))
        xt = _merge(xt, oa, ob, z, w_branch_a[li].astype(BF16), w_branch_b[li].astype(BF16),
                    w_out[li].astype(BF16))
        xt = _ffn(xt, norm_ffn_g[li].reshape(1, D_MODEL), w_ffn_gate[li].astype(BF16),
                  w_ffn_up[li].astype(BF16), w_ffn_down[li].astype(BF16))
        xt = _ple(li, xt, pt, norm_ple_g[li].reshape(1, D_MODEL), w_ple_gate[li].astype(BF16),
                  w_ple_proj[li].astype(BF16))
    return xt.reshape(BATCH, SEQ, D_MODEL)
```

```python
import functools
import math

import jax
import jax.numpy as jnp
from jax import lax
from jax.experimental import pallas as pl
from jax.experimental.pallas import tpu as pltpu

F32 = jnp.float32
BF16 = jnp.bfloat16

D_MODEL = 1024
BATCH = 8
SEQ = 4096
DEPTH = 4
TOKENS = BATCH * SEQ
CHUNK = 64
PLE_DIM = 256
EPS = 1e-6
A_HEADS = 4
A_DK = 128
A_DV = 128
A_WIDTH = A_HEADS * A_DV
B_HEADS = 8
B_HD = 64
B_WIDTH = B_HEADS * B_HD
B_LEFT_CHUNKS = 8
B_PAD = B_LEFT_CHUNKS * CHUNK
B_BAND = B_PAD + CHUNK
MAX_REL = 128
D_FF = 2816
D_IN = 5632
NEG_BIG = -1e30
LOG2E = math.log2(math.e)

COL_W = 512
W_QA, W_FA, W_IA, W_GA, W_QB, W_KB, W_VB, W_GATE_A, W_GATE_B = 0, 1, 2, 3, 4, 5, 6, 7, 9
N_W_COLS = D_IN // COL_W
Z_OF_W = {0: 0, 2: 1, 3: 2, 4: 3, 5: 4, 6: 5, 7: 6, 8: 7, 9: 8, 10: 9}
Z_QA, Z_IA, Z_GA, Z_QB, Z_KB, Z_VB, Z_GATE_A, Z_GATE_B = 0, 1, 2, 3, 4, 5, 6, 8
Z_WIDTH = D_IN - COL_W

SUBLANES = 8
LANES = 128
VMEM_LIMIT_BYTES = 56 * 1024 * 1024

INPROJ_TM = 512
HGRN_ROWS = 512
ATTN_ROWS = B_PAD
ATTN_QROWS = 2 * CHUNK
ATTN_WIN = B_PAD + ATTN_QROWS
MERGE_TM = 1024
FFN_TM = 512
PLE_TM = 1024
SUB = SUBLANES
N_SUB = CHUNK // SUB
BLK = 16
N_BLK = CHUNK // BLK
SAFE_BLOCK_NATS = 64.0


def _params(*sem):
    return pltpu.CompilerParams(dimension_semantics=sem, vmem_limit_bytes=VMEM_LIMIT_BYTES)


def _resident(shape):
    nd = len(shape)
    return pl.BlockSpec(shape, lambda *_: (0,) * nd, pipeline_mode=pl.Buffered(1))


def _rmsnorm_bf16(x, g):
    ms = jnp.mean(x * x, axis=-1, keepdims=True)
    return (x * lax.rsqrt(ms + EPS) * g).astype(BF16)


def _sigmoid(x):
    return 1.0 / (1.0 + jnp.exp(-x))


def _log_sigmoid(x):
    return jnp.minimum(x, 0.0) - jnp.log1p(jnp.exp(-jnp.abs(x)))


def _head_ones():
    r_i = lax.broadcasted_iota(jnp.int32, (LANES, LANES), 0)
    c_i = lax.broadcasted_iota(jnp.int32, (LANES, LANES), 1)
    return jnp.where((r_i < B_HD) == (c_i < B_HD), 1.0, 0.0).astype(BF16)


def _pair_head_rms(t, g_pair, ones_blk):
    sq = t * t
    hi = sq.astype(BF16)
    lo = (sq - hi.astype(F32)).astype(BF16)
    ssum = (jnp.dot(hi, ones_blk, preferred_element_type=F32)
            + jnp.dot(lo, ones_blk, preferred_element_type=F32))
    return t * lax.rsqrt(ssum * (1.0 / B_HD) + EPS) * g_pair


def _log_forget(li, f_logit, lbl):
    e = jnp.exp(lbl - jnp.max(lbl, axis=0, keepdims=True))
    soft = e / jnp.sum(e, axis=0, keepdims=True)
    lb = jnp.zeros((1, COL_W), F32)
    for r in range(1, li + 1):
        lb = lb + soft[r:r + 1, :]
    log_lb = jnp.log(lb)
    c = jnp.log1p(-lb) + _log_sigmoid(f_logit)
    mx = jnp.maximum(log_lb, c)
    return mx + jnp.log1p(jnp.exp(-jnp.abs(log_lb - c)))


def _inproj_kernel(li, x_ref, g_ref, w_ref, lbl_ref, qg_ref, kg_ref, z_ref, logf_ref):
    h = _rmsnorm_bf16(x_ref[...], g_ref[...])
    ones_blk = _head_ones()

    def heads_normed(acc, g_pair, scale):
        tiles = [_pair_head_rms(acc[:, p * LANES:(p + 1) * LANES], g_pair, ones_blk) * scale
                 for p in range(COL_W // LANES)]
        return jnp.concatenate(tiles, axis=1)

    for cb in range(N_W_COLS):
        acc = jnp.dot(h, w_ref[:, cb * COL_W:(cb + 1) * COL_W], preferred_element_type=F32)
        if cb == W_FA:
            logf_ref[...] = _log_forget(li, acc, lbl_ref[...])
            continue
        if cb == W_GA:
            acc = acc * _sigmoid(acc)
        elif cb == W_QB:
            acc = heads_normed(acc, qg_ref[...], B_HD ** -0.5)
        elif cb == W_KB:
            acc = heads_normed(acc, kg_ref[...], 1.0)
        zc = Z_OF_W[cb]
        z_ref[:, zc * COL_W:(zc + 1) * COL_W] = acc.astype(BF16)


def _inproj(li, x, g, w, lb_logits, qg2, kg2):
    tm = INPROJ_TM
    return pl.pallas_call(
        functools.partial(_inproj_kernel, li),
        grid=(TOKENS // tm,),
        in_specs=[
            pl.BlockSpec((tm, D_MODEL), lambda i: (i, 0)),
            _resident((1, D_MODEL)),
            _resident((D_MODEL, D_IN)),
            _resident((DEPTH, COL_W)),
            _resident((1, LANES)),
            _resident((1, LANES)),
        ],
        out_specs=[
            pl.BlockSpec((tm, Z_WIDTH), lambda i: (i, 0)),
            pl.BlockSpec((tm, COL_W), lambda i: (i, 0)),
        ],
        out_shape=[
            jax.ShapeDtypeStruct((TOKENS, Z_WIDTH), BF16),
            jax.ShapeDtypeStruct((TOKENS, COL_W), F32),
        ],
        compiler_params=_params("arbitrary"),
        name="inproj",
    )(x, g, w, lb_logits, qg2, kg2)


def _split3_bf16(v):
    hi = v.astype(BF16)
    r1 = v - hi.astype(F32)
    mid = r1.astype(BF16)
    lo = (r1 - mid.astype(F32)).astype(BF16)
    return hi, mid, lo


def _chunk_cumsum(lf):
    row = lax.broadcasted_iota(jnp.int32, (CHUNK, CHUNK), 0)
    col = lax.broadcasted_iota(jnp.int32, (CHUNK, CHUNK), 1)
    tril = jnp.where(row >= col, 1.0, 0.0).astype(BF16)
    hi, mid, lo = _split3_bf16(lf)
    return (jnp.dot(tril, hi, preferred_element_type=F32)
            + jnp.dot(tril, mid, preferred_element_type=F32)
            + jnp.dot(tril, lo, preferred_element_type=F32))


def _hgrn_head_chunk_fast(qs, lf, iv, st):
    b = _chunk_cumsum(lf) * LOG2E
    k = 1.0 - jnp.exp(lf)
    b_last = b[CHUNK - 1:CHUNK, :]

    o = lax.dot_general((qs * jnp.exp2(b)).astype(BF16), st.astype(BF16),
                        (((1,), (1,)), ((), ())), preferred_element_type=F32)

    zero_blk = jnp.zeros((BLK, A_DK), F32)
    c_rows = [jnp.zeros((1, A_DK), F32)] + [b[BLK * j - 1:BLK * j, :] for j in range(1, N_BLK)]
    cref = jnp.concatenate([jnp.broadcast_to(c, (BLK, A_DK)) for c in c_rows], axis=0)
    q_all = qs * jnp.exp2(b - cref)
    q_parts, k_parts = [], []
    for j in range(N_BLK):
        hi_row = BLK * (j + 1)
        q_parts.append(jnp.concatenate(
            [zero_blk] * j + [q_all[BLK * j:hi_row, :]] + [zero_blk] * (N_BLK - 1 - j), axis=0))
        kd = k[:hi_row, :] * jnp.exp2(c_rows[j] - b[:hi_row, :])
        if hi_row < CHUNK:
            kd = jnp.concatenate([kd, jnp.zeros((CHUNK - hi_row, A_DK), F32)], axis=0)
        k_parts.append(kd)
    q_cat = jnp.concatenate(q_parts, axis=1).astype(BF16)
    k_cat = jnp.concatenate(k_parts, axis=1).astype(BF16)
    attn = lax.dot_general(q_cat, k_cat, (((1,), (1,)), ((), ())),
                           preferred_element_type=F32)
    row = lax.broadcasted_iota(jnp.int32, (CHUNK, CHUNK), 0)
    col = lax.broadcasted_iota(jnp.int32, (CHUNK, CHUNK), 1)
    attn = jnp.where(row >= col, attn, 0.0)
    o = o + jnp.dot(attn.astype(BF16), iv, preferred_element_type=F32)

    k_dec = (k * jnp.exp2(b_last - b)).astype(BF16)
    st_new = st * jnp.exp2(b_last) + lax.dot_general(
        iv, k_dec, (((0,), (0,)), ((), ())), preferred_element_type=F32)
    return o, st_new


def _hgrn_head_chunk_exact(qs, lf, iv, st):
    b = _chunk_cumsum(lf)
    k = 1.0 - jnp.exp(lf)
    b_last = b[CHUNK - 1:CHUNK, :]

    o = lax.dot_general((qs * jnp.exp(b)).astype(BF16), st.astype(BF16),
                        (((1,), (1,)), ((), ())), preferred_element_type=F32)

    zero_blk = jnp.zeros((SUB, A_DK), F32)
    c_rows = [None] + [b[SUB * i - 1:SUB * i, :] for i in range(1, N_SUB)]
    cref = jnp.concatenate(
        [zero_blk] + [jnp.broadcast_to(c_rows[i], (SUB, A_DK)) for i in range(1, N_SUB)], axis=0)
    q_all = qs * jnp.exp(b - cref)
    q_parts, k_parts = [], []
    for i in range(1, N_SUB):
        q_parts.append(jnp.concatenate(
            [zero_blk] * i + [q_all[SUB * i:SUB * (i + 1), :]] + [zero_blk] * (N_SUB - 1 - i), axis=0))
        kd = k[:SUB * i, :] * jnp.exp(c_rows[i] - b[:SUB * i, :])
        k_parts.append(jnp.concatenate([kd, jnp.zeros((CHUNK - SUB * i, A_DK), F32)], axis=0))
    q_cat = jnp.concatenate(q_parts, axis=1).astype(BF16)
    k_cat = jnp.concatenate(k_parts, axis=1).astype(BF16)
    attn = lax.dot_general(q_cat, k_cat, (((1,), (1,)), ((), ())),
                           preferred_element_type=F32)

    t_loc = lax.broadcasted_iota(jnp.int32, (SUB, CHUNK), 0)
    lane = lax.broadcasted_iota(jnp.int32, (SUB, CHUNK), 1)
    diag_blocks = []
    for i in range(N_SUB):
        qi = qs[SUB * i:SUB * (i + 1), :]
        bi = b[SUB * i:SUB * (i + 1), :]
        blk = jnp.zeros((SUB, CHUNK), F32)
        for jj in range(SUB):
            s = SUB * i + jj
            p = qi * k[s:s + 1, :] * jnp.exp(bi - b[s:s + 1, :])
            colsum = jnp.sum(p, axis=-1, keepdims=True)
            blk = jnp.where((lane == s) & (t_loc >= jj), colsum, blk)
        diag_blocks.append(blk)
    attn = attn + jnp.concatenate(diag_blocks, axis=0)

    o = o + jnp.dot(attn.astype(BF16), iv, preferred_element_type=F32)

    k_dec = (k * jnp.exp(b_last - b)).astype(BF16)
    st_new = st * jnp.exp(b_last) + lax.dot_general(
        iv, k_dec, (((0,), (0,)), ((), ())), preferred_element_type=F32)
    return o, st_new


def _hgrn_kernel(q_ref, lf_ref, i_ref, gs_ref, og_ref, out_ref, st_ref):
    @pl.when(pl.program_id(1) == 0)
    def _():
        st_ref[...] = jnp.zeros_like(st_ref)

    n_blocks = HGRN_ROWS // BLK
    blk_of_row = lax.shift_right_logical(
        lax.broadcasted_iota(jnp.int32, (n_blocks, HGRN_ROWS), 1), int(math.log2(BLK)))
    ind = jnp.where(blk_of_row == lax.broadcasted_iota(jnp.int32, (n_blocks, HGRN_ROWS), 0),
                    1.0, 0.0).astype(BF16)
    blk_sums = jnp.dot(ind, lf_ref[...].astype(BF16), preferred_element_type=F32)
    safe = jnp.min(blk_sums) >= -SAFE_BLOCK_NATS

    def run(head_chunk):
        def chunk_body(c, carry):
            r0 = pl.multiple_of(c * CHUNK, CHUNK)
            rows = pl.ds(r0, CHUNK)
            for h in range(A_HEADS):
                cols = slice(h * A_DK, (h + 1) * A_DK)
                qs = q_ref[rows, cols].astype(F32) * (A_DK ** -0.5)
                o, st_new = head_chunk(qs, lf_ref[rows, cols], i_ref[rows, cols], st_ref[h])
                st_ref[h] = st_new
                ms = jnp.mean(o * o, axis=-1, keepdims=True)
                y = o * lax.rsqrt(ms + EPS) * og_ref[...] * gs_ref[rows, cols].astype(F32)
                out_ref[rows, cols] = y.astype(BF16)
            return carry

        lax.fori_loop(0, HGRN_ROWS // CHUNK, chunk_body, 0)

    @pl.when(safe)
    def _():
        run(_hgrn_head_chunk_fast)

    @pl.when(jnp.logical_not(safe))
    def _():
        run(_hgrn_head_chunk_exact)


def _hgrn(z, logf, onorm_g):
    nb = SEQ // HGRN_ROWS

    def zcol(cb):
        return pl.BlockSpec((HGRN_ROWS, COL_W), lambda b, n: (b * nb + n, cb))

    return pl.pallas_call(
        _hgrn_kernel,
        grid=(BATCH, nb),
        in_specs=[
            zcol(Z_QA),
            pl.BlockSpec((HGRN_ROWS, COL_W), lambda b, n: (b * nb + n, 0)),
            zcol(Z_IA),
            zcol(Z_GA),
            _resident((1, A_DV)),
        ],
        out_specs=pl.BlockSpec((HGRN_ROWS, A_WIDTH), lambda b, n: (b * nb + n, 0)),
        out_shape=jax.ShapeDtypeStruct((TOKENS, A_WIDTH), BF16),
        scratch_shapes=[pltpu.VMEM((A_HEADS, A_DV, A_DK), F32)],
        compiler_params=_params("arbitrary", "arbitrary"),
        name="hgrn2",
    )(z, logf, z, z, onorm_g)


def _attn_kernel(q_ref, kc_ref, vp_ref, vc_ref, bias_ref, out_ref, kw_ref, vw_ref):
    n = pl.program_id(1)
    rows = ATTN_ROWS

    @pl.when(n == 0)
    def _():
        kw_ref[0:rows, :] = jnp.zeros((rows, B_WIDTH), BF16)

    @pl.when(n > 0)
    def _():
        kw_ref[0:rows, :] = kw_ref[rows:2 * rows, :]

    kw_ref[rows:2 * rows, :] = kc_ref[...]
    vw_ref[0:rows, :] = vp_ref[...]
    vw_ref[rows:2 * rows, :] = vc_ref[...]

    win_col = lax.broadcasted_iota(jnp.int32, (1, ATTN_WIN), 1)
    first_q = lax.broadcasted_iota(jnp.int32, (ATTN_QROWS, LANES), 1) < B_HD

    def unit_body(u, carry):
        r0 = pl.multiple_of(u * ATTN_QROWS, ATTN_QROWS)
        qrows = pl.ds(r0, ATTN_QROWS)
        wrows = pl.ds(r0, ATTN_WIN)
        seq_mask = jnp.where((n > 0) | (win_col + r0 >= rows), 0.0, NEG_BIG)
        for p in range(B_HEADS // 2):
            cols = slice(p * LANES, (p + 1) * LANES)
            qp = q_ref[qrows, cols]
            zero = jnp.zeros_like(qp)
            qq = jnp.concatenate([jnp.where(first_q, qp, zero), jnp.where(first_q, zero, qp)], axis=0)
            s = lax.dot_general(qq, kw_ref[wrows, cols], (((1,), (1,)), ((), ())),
                                preferred_element_type=F32)
            s = s + bias_ref[p] + seq_mask
            m = jnp.max(s, axis=-1, keepdims=True)
            e = jnp.exp(s - m)
            l = jnp.sum(e, axis=-1, keepdims=True)
            o = jnp.dot(e.astype(BF16), vw_ref[wrows, cols], preferred_element_type=F32) / l
            out_ref[qrows, cols] = jnp.where(
                first_q, o[:ATTN_QROWS], o[ATTN_QROWS:]).astype(BF16)
        return carry

    lax.fori_loop(0, rows // ATTN_QROWS, unit_body, 0)


def _band_bias(rel_bias):
    rb = rel_bias.astype(F32)
    n_far = (CHUNK - 1) + (B_PAD - MAX_REL) + 1
    far = jnp.broadcast_to(rb[:, 2 * MAX_REL:], (B_HEADS, n_far))
    near = rb[:, MAX_REL - (CHUNK - 1):2 * MAX_REL][:, ::-1]
    tb = jnp.concatenate([far, near], axis=1)
    band = jnp.stack([tb[:, CHUNK - 1 - c:CHUNK - 1 - c + B_BAND] for c in range(CHUNK)], axis=1)
    lo = jnp.pad(band, ((0, 0), (0, 0), (0, CHUNK)), constant_values=NEG_BIG)
    hi = jnp.pad(band, ((0, 0), (0, 0), (CHUNK, 0)), constant_values=NEG_BIG)
    unit = jnp.concatenate([lo, hi], axis=1)
    return unit.reshape(B_HEADS // 2, 2 * ATTN_QROWS, ATTN_WIN)


def _attn(z, bias):
    rows = ATTN_ROWS
    nb = SEQ // rows

    def cur(cb):
        return pl.BlockSpec((rows, COL_W), lambda b, n: (b * nb + n, cb))

    def prev(cb):
        return pl.BlockSpec((rows, COL_W), lambda b, n: (b * nb + jnp.maximum(n - 1, 0), cb))

    return pl.pallas_call(
        _attn_kernel,
        grid=(BATCH, nb),
        in_specs=[
            cur(Z_QB), cur(Z_KB), prev(Z_VB), cur(Z_VB),
            _resident((B_HEADS // 2, 2 * ATTN_QROWS, ATTN_WIN)),
        ],
        out_specs=pl.BlockSpec((rows, B_WIDTH), lambda b, n: (b * nb + n, 0)),
        out_shape=jax.ShapeDtypeStruct((TOKENS, B_WIDTH), BF16),
        scratch_shapes=[
            pltpu.VMEM((2 * rows, B_WIDTH), BF16),
            pltpu.VMEM((2 * rows, B_WIDTH), BF16),
        ],
        compiler_params=_params("arbitrary", "arbitrary"),
        name="band_attn",
    )(z, z, z, z, bias)


def _merge_kernel(x_ref, oa_ref, ob_ref, ga0_ref, ga1_ref, gb0_ref, gb1_ref,
                  wa_ref, wb_ref, wo_ref, out_ref):
    ga = jnp.concatenate([ga0_ref[...], ga1_ref[...]], axis=1).astype(F32)
    gb = jnp.concatenate([gb0_ref[...], gb1_ref[...]], axis=1).astype(F32)
    pa = jnp.dot(oa_ref[...], wa_ref[...], preferred_element_type=F32)
    pb = jnp.dot(ob_ref[...], wb_ref[...], preferred_element_type=F32)
    merged = (_sigmoid(ga) * pa + _sigmoid(gb) * pb).astype(BF16)
    out_ref[...] = x_ref[...] + jnp.dot(merged, wo_ref[...], preferred_element_type=F32)


def _merge(x, oa, ob, z, wa, wb, wo):
    tm = MERGE_TM

    def zcol(cb):
        return pl.BlockSpec((tm, COL_W), lambda i: (i, cb))

    return pl.pallas_call(
        _merge_kernel,
        grid=(TOKENS // tm,),
        in_specs=[
            pl.BlockSpec((tm, D_MODEL), lambda i: (i, 0)),
            pl.BlockSpec((tm, A_WIDTH), lambda i: (i, 0)),
            pl.BlockSpec((tm, B_WIDTH), lambda i: (i, 0)),
            zcol(Z_GATE_A), zcol(Z_GATE_A + 1), zcol(Z_GATE_B), zcol(Z_GATE_B + 1),
            _resident((A_WIDTH, D_MODEL)),
            _resident((B_WIDTH, D_MODEL)),
            _resident((D_MODEL, D_MODEL)),
        ],
        out_specs=pl.BlockSpec((tm, D_MODEL), lambda i: (i, 0)),
        out_shape=jax.ShapeDtypeStruct((TOKENS, D_MODEL), F32),
        compiler_params=_params("arbitrary"),
        name="merge_out",
    )(x, oa, ob, z, z, z, z, wa, wb, wo)


def _ffn_kernel(x_ref, g_ref, wg_ref, wu_ref, wd_ref, out_ref):
    x = x_ref[...]
    h = _rmsnorm_bf16(x, g_ref[...])
    gate = jnp.dot(h, wg_ref[...], preferred_element_type=F32)
    up = jnp.dot(h, wu_ref[...], preferred_element_type=F32)
    act = (gate * _sigmoid(gate) * up).astype(BF16)
    out_ref[...] = x + jnp.dot(act, wd_ref[...], preferred_element_type=F32)


def _ffn(x, g, wg, wu, wd):
    tm = FFN_TM
    return pl.pallas_call(
        _ffn_kernel,
        grid=(TOKENS // tm,),
        in_specs=[
            pl.BlockSpec((tm, D_MODEL), lambda i: (i, 0)),
            _resident((1, D_MODEL)),
            _resident((D_MODEL, D_FF)),
            _resident((D_MODEL, D_FF)),
            _resident((D_FF, D_MODEL)),
        ],
        out_specs=pl.BlockSpec((tm, D_MODEL), lambda i: (i, 0)),
        out_shape=jax.ShapeDtypeStruct((TOKENS, D_MODEL), F32),
        compiler_params=_params("arbitrary"),
        name="ffn",
    )(x, g, wg, wu, wd)


def _ple_kernel(x_ref, p_ref, g_ref, wg_ref, wp_ref, out_ref):
    x = x_ref[...]
    h = _rmsnorm_bf16(x, g_ref[...])
    gate = _sigmoid(jnp.dot(h, wg_ref[...], preferred_element_type=F32))
    emb = jnp.dot(p_ref[...].astype(BF16), wp_ref[...], preferred_element_type=F32)
    out_ref[...] = x + gate * emb


def _ple(li, x, p, g, wg, wp):
    tm = PLE_TM
    return pl.pallas_call(
        _ple_kernel,
        grid=(TOKENS // tm,),
        in_specs=[
            pl.BlockSpec((tm, D_MODEL), lambda i: (i, 0)),
            pl.BlockSpec((None, tm, PLE_DIM), lambda i: (li, i, 0)),
            _resident((1, D_MODEL)),
            _resident((D_MODEL, D_MODEL)),
            _resident((PLE_DIM, D_MODEL)),
        ],
        out_specs=pl.BlockSpec((tm, D_MODEL), lambda i: (i, 0)),
        out_shape=jax.ShapeDtypeStruct((TOKENS, D_MODEL), F32),
        compiler_params=_params("arbitrary"),
        name="ple",
    )(x, p, g, wg, wp)


def kernel(x, p, norm_mix_g, w_in, hgrn_lb_logits, hgrn_onorm_g, attn_qnorm_g, attn_knorm_g,
           attn_rel_bias, w_branch_a, w_branch_b, w_out, norm_ffn_g, w_ffn_gate, w_ffn_up,
           w_ffn_down, norm_ple_g, w_ple_gate, w_ple_proj):
    xt = x.reshape(TOKENS, D_MODEL)
    pt = p.reshape(DEPTH, TOKENS, PLE_DIM)
    for li in range(DEPTH):
        qg2 = jnp.concatenate([attn_qnorm_g[li], attn_qnorm_g[li]]).reshape(1, LANES)
        kg2 = jnp.concatenate([attn_knorm_g[li], attn_knorm_g[li]]).reshape(1, LANES)
        z, logf = _inproj(li, xt, norm_mix_g[li].reshape(1, D_MODEL), w_in[li].astype(BF16),
                          hgrn_lb_logits, qg2, kg2)
        oa = _hgrn(z, logf, hgrn_onorm_g[li].reshape(1, A_DV))
        ob = _attn(z, _band_bias(attn_rel_bias[li]))
        xt = _merge(xt, oa, ob, z, w_branch_a[li].astype(BF16), w_branch_b[li].astype(BF16),
                    w_out[li].astype(BF16))
        xt = _ffn(xt, norm_ffn_g[li].reshape(1, D_MODEL), w_ffn_gate[li].astype(BF16),
                  w_ffn_up[li].astype(BF16), w_ffn_down[li].astype(BF16))
        xt = _ple(li, xt, pt, norm_ple_g[li].reshape(1, D_MODEL), w_ple_gate[li].astype(BF16),
                  w_ple_proj[li].astype(BF16))
    return xt.reshape(BATCH, SEQ, D_MODEL)
```

```python
import functools
import math

import jax
import jax.numpy as jnp
from jax import lax
from jax.experimental import pallas as pl
from jax.experimental.pallas import tpu as pltpu

F32 = jnp.float32
BF16 = jnp.bfloat16

D_MODEL = 1024
BATCH = 8
SEQ = 4096
DEPTH = 4
TOKENS = BATCH * SEQ
CHUNK = 64
PLE_DIM = 256
EPS = 1e-6
A_HEADS = 4
A_DK = 128
A_DV = 128
A_WIDTH = A_HEADS * A_DV
B_HEADS = 8
B_HD = 64
B_WIDTH = B_HEADS * B_HD
B_LEFT_CHUNKS = 8
B_PAD = B_LEFT_CHUNKS * CHUNK
B_BAND = B_PAD + CHUNK
MAX_REL = 128
D_FF = 2816
D_IN = 5632
NEG_BIG = -1e30
LOG2E = math.log2(math.e)
ATTN_Q_SCALE = B_HD ** -0.5 * LOG2E

COL_W = 512
W_QA, W_FA, W_IA, W_GA, W_QB, W_KB, W_VB, W_GATE_A, W_GATE_B = 0, 1, 2, 3, 4, 5, 6, 7, 9
N_W_COLS = D_IN // COL_W
Z_OF_W = {0: 0, 2: 1, 3: 2, 4: 3, 5: 4, 6: 5, 7: 6, 8: 7, 9: 8, 10: 9}
Z_QA, Z_IA, Z_GA, Z_QB, Z_KB, Z_VB, Z_GATE_A, Z_GATE_B = 0, 1, 2, 3, 4, 5, 6, 8
Z_WIDTH = D_IN - COL_W

SUBLANES = 8
LANES = 128
VMEM_LIMIT_BYTES = 56 * 1024 * 1024

INPROJ_TM = 512
HGRN_ROWS = 512
HGRN_NC = 4
ATTN_ROWS = B_PAD
ATTN_QROWS = 2 * CHUNK
ATTN_WIN = B_PAD + ATTN_QROWS
MERGE_TM = 1024
FFN_TM = 512
PLE_TM = 1024
SUB = SUBLANES
N_SUB = CHUNK // SUB
BLK = 16
N_BLK = CHUNK // BLK
SAFE_BLOCK_NATS = 64.0


def _params(*sem):
    return pltpu.CompilerParams(dimension_semantics=sem, vmem_limit_bytes=VMEM_LIMIT_BYTES)


def _resident(shape):
    nd = len(shape)
    return pl.BlockSpec(shape, lambda *_: (0,) * nd, pipeline_mode=pl.Buffered(1))


def _rmsnorm_bf16(x, g):
    ms = jnp.mean(x * x, axis=-1, keepdims=True)
    return (x * lax.rsqrt(ms + EPS) * g).astype(BF16)


def _sigmoid(x):
    return 1.0 / (1.0 + jnp.exp(-x))


def _head_ones():
    r_i = lax.broadcasted_iota(jnp.int32, (LANES, LANES), 0)
    c_i = lax.broadcasted_iota(jnp.int32, (LANES, LANES), 1)
    return jnp.where((r_i < B_HD) == (c_i < B_HD), 1.0, 0.0).astype(BF16)


def _pair_head_rms(t, g_pair, ones_blk):
    sq = t * t
    hi = sq.astype(BF16)
    lo = (sq - hi.astype(F32)).astype(BF16)
    ssum = (jnp.dot(hi, ones_blk, preferred_element_type=F32)
            + jnp.dot(lo, ones_blk, preferred_element_type=F32))
    return t * lax.rsqrt(ssum * (1.0 / B_HD) + EPS) * g_pair


def _softplus_neg_abs(d):
    return jnp.log(1.0 + jnp.exp(-jnp.abs(d)))


def _log_forget_consts(li, lbl):
    e = jnp.exp(lbl - jnp.max(lbl, axis=0, keepdims=True))
    soft = e / jnp.sum(e, axis=0, keepdims=True)
    lb = jnp.zeros((1, COL_W), F32)
    for r in range(1, li + 1):
        lb = lb + soft[r:r + 1, :]
    return jnp.log(lb), jnp.log1p(-lb)


def _log_forget(f_logit, log_lb, log_1m_lb):
    c = log_1m_lb + jnp.minimum(f_logit, 0.0) - _softplus_neg_abs(f_logit)
    return jnp.maximum(log_lb, c) + _softplus_neg_abs(log_lb - c)


def _inproj_kernel(li, x_ref, g_ref, w_ref, lbl_ref, qg_ref, kg_ref, z_ref, logf_ref):
    h = _rmsnorm_bf16(x_ref[...], g_ref[...])
    ones_blk = _head_ones()
    log_lb, log_1m_lb = _log_forget_consts(li, lbl_ref[...])
    n_slices = 4
    slice_rows = INPROJ_TM // n_slices

    def heads_normed(acc, g_pair, scale):
        tiles = [_pair_head_rms(acc[:, p * LANES:(p + 1) * LANES], g_pair, ones_blk) * scale
                 for p in range(COL_W // LANES)]
        return jnp.concatenate(tiles, axis=1)

    def finish(cb, acc):
        if cb == W_FA:
            logf_ref[...] = acc
            return
        if cb == W_GA:
            acc = acc * _sigmoid(acc)
        elif cb == W_QB:
            acc = heads_normed(acc, qg_ref[...], ATTN_Q_SCALE)
        elif cb == W_KB:
            acc = heads_normed(acc, kg_ref[...], 1.0)
        zc = Z_OF_W[cb]
        z_ref[:, zc * COL_W:(zc + 1) * COL_W] = acc.astype(BF16)

    def forget_slice(s):
        rows = slice(s * slice_rows, (s + 1) * slice_rows)
        logf_ref[rows, :] = _log_forget(logf_ref[rows, :], log_lb, log_1m_lb)

    pending = None
    for cb in range(N_W_COLS):
        acc = jnp.dot(h, w_ref[:, cb * COL_W:(cb + 1) * COL_W], preferred_element_type=F32)
        if pending is not None:
            finish(*pending)
        pending = (cb, acc)
        s = cb - (W_FA + 2)
        if 0 <= s < n_slices:
            forget_slice(s)
    finish(*pending)


def _inproj(li, x, g, w, lb_logits, qg2, kg2):
    tm = INPROJ_TM
    return pl.pallas_call(
        functools.partial(_inproj_kernel, li),
        grid=(TOKENS // tm,),
        in_specs=[
            pl.BlockSpec((tm, D_MODEL), lambda i: (i, 0)),
            _resident((1, D_MODEL)),
            _resident((D_MODEL, D_IN)),
            _resident((DEPTH, COL_W)),
            _resident((1, LANES)),
            _resident((1, LANES)),
        ],
        out_specs=[
            pl.BlockSpec((tm, Z_WIDTH), lambda i: (i, 0)),
            pl.BlockSpec((tm, COL_W), lambda i: (i, 0)),
        ],
        out_shape=[
            jax.ShapeDtypeStruct((TOKENS, Z_WIDTH), BF16),
            jax.ShapeDtypeStruct((TOKENS, COL_W), F32),
        ],
        compiler_params=_params("arbitrary"),
        name="inproj",
    )(x, g, w, lb_logits, qg2, kg2)


def _split3_bf16(v):
    hi = v.astype(BF16)
    r1 = v - hi.astype(F32)
    mid = r1.astype(BF16)
    lo = (r1 - mid.astype(F32)).astype(BF16)
    return hi, mid, lo


def _chunk_cumsum(lf):
    row = lax.broadcasted_iota(jnp.int32, (CHUNK, CHUNK), 0)
    col = lax.broadcasted_iota(jnp.int32, (CHUNK, CHUNK), 1)
    tril = jnp.where(row >= col, 1.0, 0.0).astype(BF16)
    hi, mid, lo = _split3_bf16(lf)
    return (jnp.dot(tril, hi, preferred_element_type=F32)
            + jnp.dot(tril, mid, preferred_element_type=F32)
            + jnp.dot(tril, lo, preferred_element_type=F32))


def _hgrn_rows_fast(q_ref, lf_ref, i_ref, gs_ref, og_ref, out_ref, st_ref, r0):
    nrow = HGRN_NC * CHUNK
    rows = pl.ds(r0, nrow)
    lf = lf_ref[rows, :]
    row = lax.broadcasted_iota(jnp.int32, (nrow, nrow), 0)
    col = lax.broadcasted_iota(jnp.int32, (nrow, nrow), 1)
    shift = int(math.log2(CHUNK))
    same_chunk = lax.shift_right_logical(row, shift) == lax.shift_right_logical(col, shift)
    tril = jnp.where((row >= col) & same_chunk, 1.0, 0.0).astype(BF16)
    hi, mid, lo = _split3_bf16(lf)
    b = (jnp.dot(tril, hi, preferred_element_type=F32)
         + jnp.dot(tril, mid, preferred_element_type=F32)
         + jnp.dot(tril, lo, preferred_element_type=F32)) * LOG2E
    k = 1.0 - jnp.exp(lf)
    qs = q_ref[rows, :].astype(F32) * (A_DK ** -0.5)
    q_dec = (qs * jnp.exp2(b)).astype(BF16)

    zero_row = jnp.zeros((1, A_WIDTH), F32)
    c_rows = [[zero_row if j == 0 else b[c * CHUNK + BLK * j - 1:c * CHUNK + BLK * j, :]
               for j in range(N_BLK)] for c in range(HGRN_NC)]
    cref = jnp.concatenate([jnp.broadcast_to(c_rows[c][j], (BLK, A_WIDTH))
                            for c in range(HGRN_NC) for j in range(N_BLK)], axis=0)
    q_all = (qs * jnp.exp2(b - cref)).astype(BF16)
    b_last = [b[(c + 1) * CHUNK - 1:(c + 1) * CHUNK, :] for c in range(HGRN_NC)]
    k_dec = jnp.concatenate(
        [k[c * CHUNK:(c + 1) * CHUNK, :] * jnp.exp2(b_last[c] - b[c * CHUNK:(c + 1) * CHUNK, :])
         for c in range(HGRN_NC)], axis=0).astype(BF16)
    k_blk = [[(k[c * CHUNK:c * CHUNK + BLK * (j + 1), :]
               * jnp.exp2(c_rows[c][j] - b[c * CHUNK:c * CHUNK + BLK * (j + 1), :])).astype(BF16)
              for j in range(N_BLK)] for c in range(HGRN_NC)]
    iv = i_ref[rows, :]
    gs = gs_ref[rows, :]
    causal = (lax.broadcasted_iota(jnp.int32, (CHUNK, CHUNK), 0)
              >= lax.broadcasted_iota(jnp.int32, (CHUNK, CHUNK), 1))

    def zeros16(n):
        return [jnp.zeros((BLK, A_DK), BF16)] * n

    units = [(h, c) for h in range(A_HEADS) for c in range(HGRN_NC)]

    def hcols(h):
        return slice(h * A_DK, (h + 1) * A_DK)

    def crows(c):
        return slice(c * CHUNK, (c + 1) * CHUNK)

    attn, upd = {}, {}
    for h, c in units:
        hc = hcols(h)
        q_cat = jnp.concatenate(
            [jnp.concatenate(zeros16(j) + [q_all[c * CHUNK + BLK * j:c * CHUNK + BLK * (j + 1), hc]]
                             + zeros16(N_BLK - 1 - j), axis=0) for j in range(N_BLK)], axis=1)
        k_cat = jnp.concatenate(
            [jnp.concatenate([k_blk[c][j][:, hc]] + zeros16(N_BLK - 1 - j), axis=0)
             for j in range(N_BLK)], axis=1)
        attn[h, c] = lax.dot_general(q_cat, k_cat, (((1,), (1,)), ((), ())),
                                     preferred_element_type=F32)
    for h, c in units:
        upd[h, c] = lax.dot_general(iv[crows(c), hcols(h)], k_dec[crows(c), hcols(h)],
                                    (((0,), (0,)), ((), ())), preferred_element_type=F32)
    o_state = {}
    for h in range(A_HEADS):
        st = st_ref[h]
        for c in range(HGRN_NC):
            o_state[h, c] = lax.dot_general(q_dec[crows(c), hcols(h)], st.astype(BF16),
                                            (((1,), (1,)), ((), ())), preferred_element_type=F32)
            st = st * jnp.exp2(b_last[c][:, hcols(h)]) + upd[h, c]
        st_ref[h] = st
    for h, c in units:
        a = jnp.where(causal, attn[h, c], 0.0).astype(BF16)
        o = jnp.dot(a, iv[crows(c), hcols(h)], preferred_element_type=F32) + o_state[h, c]
        ms = jnp.mean(o * o, axis=-1, keepdims=True)
        y = o * lax.rsqrt(ms + EPS) * og_ref[...] * gs[crows(c), hcols(h)].astype(F32)
        out_ref[pl.ds(r0 + c * CHUNK, CHUNK), hcols(h)] = y.astype(BF16)


def _hgrn_head_chunk_exact(qs, lf, iv, st):
    b = _chunk_cumsum(lf)
    k = 1.0 - jnp.exp(lf)
    b_last = b[CHUNK - 1:CHUNK, :]

    o = lax.dot_general((qs * jnp.exp(b)).astype(BF16), st.astype(BF16),
                        (((1,), (1,)), ((), ())), preferred_element_type=F32)

    zero_blk = jnp.zeros((SUB, A_DK), F32)
    c_rows = [None] + [b[SUB * i - 1:SUB * i, :] for i in range(1, N_SUB)]
    cref = jnp.concatenate(
        [zero_blk] + [jnp.broadcast_to(c_rows[i], (SUB, A_DK)) for i in range(1, N_SUB)], axis=0)
    q_all = qs * jnp.exp(b - cref)
    q_parts, k_parts = [], []
    for i in range(1, N_SUB):
        q_parts.append(jnp.concatenate(
            [zero_blk] * i + [q_all[SUB * i:SUB * (i + 1), :]] + [zero_blk] * (N_SUB - 1 - i), axis=0))
        kd = k[:SUB * i, :] * jnp.exp(c_rows[i] - b[:SUB * i, :])
        k_parts.append(jnp.concatenate([kd, jnp.zeros((CHUNK - SUB * i, A_DK), F32)], axis=0))
    q_cat = jnp.concatenate(q_parts, axis=1).astype(BF16)
    k_cat = jnp.concatenate(k_parts, axis=1).astype(BF16)
    attn = lax.dot_general(q_cat, k_cat, (((1,), (1,)), ((), ())),
                           preferred_element_type=F32)

    t_loc = lax.broadcasted_iota(jnp.int32, (SUB, CHUNK), 0)
    lane = lax.broadcasted_iota(jnp.int32, (SUB, CHUNK), 1)
    diag_blocks = []
    for i in range(N_SUB):
        qi = qs[SUB * i:SUB * (i + 1), :]
        bi = b[SUB * i:SUB * (i + 1), :]
        blk = jnp.zeros((SUB, CHUNK), F32)
        for jj in range(SUB):
            s = SUB * i + jj
            p = qi * k[s:s + 1, :] * jnp.exp(bi - b[s:s + 1, :])
            colsum = jnp.sum(p, axis=-1, keepdims=True)
            blk = jnp.where((lane == s) & (t_loc >= jj), colsum, blk)
        diag_blocks.append(blk)
    attn = attn + jnp.concatenate(diag_blocks, axis=0)

    o = o + jnp.dot(attn.astype(BF16), iv, preferred_element_type=F32)

    k_dec = (k * jnp.exp(b_last - b)).astype(BF16)
    st_new = st * jnp.exp(b_last) + lax.dot_general(
        iv, k_dec, (((0,), (0,)), ((), ())), preferred_element_type=F32)
    return o, st_new


def _hgrn_kernel(q_ref, lf_ref, i_ref, gs_ref, og_ref, out_ref, st_ref):
    @pl.when(pl.program_id(1) == 0)
    def _():
        st_ref[...] = jnp.zeros_like(st_ref)

    n_blocks = HGRN_ROWS // BLK
    blk_of_row = lax.shift_right_logical(
        lax.broadcasted_iota(jnp.int32, (n_blocks, HGRN_ROWS), 1), int(math.log2(BLK)))
    ind = jnp.where(blk_of_row == lax.broadcasted_iota(jnp.int32, (n_blocks, HGRN_ROWS), 0),
                    1.0, 0.0).astype(BF16)
    blk_sums = jnp.dot(ind, lf_ref[...].astype(BF16), preferred_element_type=F32)
    safe = jnp.min(blk_sums) >= -SAFE_BLOCK_NATS

    @pl.when(safe)
    def _():
        def rows_body(it, carry):
            r0 = pl.multiple_of(it * (HGRN_NC * CHUNK), HGRN_NC * CHUNK)
            _hgrn_rows_fast(q_ref, lf_ref, i_ref, gs_ref, og_ref, out_ref, st_ref, r0)
            return carry

        lax.fori_loop(0, HGRN_ROWS // (HGRN_NC * CHUNK), rows_body, 0)

    @pl.when(jnp.logical_not(safe))
    def _():
        def chunk_body(c, carry):
            r0 = pl.multiple_of(c * CHUNK, CHUNK)
            rows = pl.ds(r0, CHUNK)
            for h in range(A_HEADS):
                cols = slice(h * A_DK, (h + 1) * A_DK)
                qs = q_ref[rows, cols].astype(F32) * (A_DK ** -0.5)
                o, st_new = _hgrn_head_chunk_exact(qs, lf_ref[rows, cols], i_ref[rows, cols], st_ref[h])
                st_ref[h] = st_new
                ms = jnp.mean(o * o, axis=-1, keepdims=True)
                y = o * lax.rsqrt(ms + EPS) * og_ref[...] * gs_ref[rows, cols].astype(F32)
                out_ref[rows, cols] = y.astype(BF16)
            return carry

        lax.fori_loop(0, HGRN_ROWS // CHUNK, chunk_body, 0)


def _hgrn(z, logf, onorm_g):
    nb = SEQ // HGRN_ROWS

    def zcol(cb):
        return pl.BlockSpec((HGRN_ROWS, COL_W), lambda b, n: (b * nb + n, cb))

    return pl.pallas_call(
        _hgrn_kernel,
        grid=(BATCH, nb),
        in_specs=[
            zcol(Z_QA),
            pl.BlockSpec((HGRN_ROWS, COL_W), lambda b, n: (b * nb + n, 0)),
            zcol(Z_IA),
            zcol(Z_GA),
            _resident((1, A_DV)),
        ],
        out_specs=pl.BlockSpec((HGRN_ROWS, A_WIDTH), lambda b, n: (b * nb + n, 0)),
        out_shape=jax.ShapeDtypeStruct((TOKENS, A_WIDTH), BF16),
        scratch_shapes=[pltpu.VMEM((A_HEADS, A_DV, A_DK), F32)],
        compiler_params=_params("arbitrary", "arbitrary"),
        name="hgrn2",
    )(z, logf, z, z, onorm_g)


def _attn_kernel(q_ref, kc_ref, vp_ref, vc_ref, bias_ref, out_ref, kw_ref, vw_ref):
    n = pl.program_id(1)
    rows = ATTN_ROWS

    @pl.when(n == 0)
    def _():
        kw_ref[0:rows, :] = jnp.zeros((rows, B_WIDTH), BF16)

    @pl.when(n > 0)
    def _():
        kw_ref[0:rows, :] = kw_ref[rows:2 * rows, :]

    kw_ref[rows:2 * rows, :] = kc_ref[...]
    vw_ref[0:rows, :] = vp_ref[...]
    vw_ref[rows:2 * rows, :] = vc_ref[...]

    win_col = lax.broadcasted_iota(jnp.int32, (1, ATTN_WIN), 1)
    first_q = lax.broadcasted_iota(jnp.int32, (ATTN_QROWS, LANES), 1) < B_HD

    def run(first_block):
        units = [(u, p) for u in range(rows // ATTN_QROWS) for p in range(B_HEADS // 2)]

        def qrows(u):
            return slice(u * ATTN_QROWS, (u + 1) * ATTN_QROWS)

        def wrows(u):
            return slice(u * ATTN_QROWS, u * ATTN_QROWS + ATTN_WIN)

        def cols(p):
            return slice(p * LANES, (p + 1) * LANES)

        def score(u, p):
            qp = q_ref[qrows(u), cols(p)]
            zero = jnp.zeros_like(qp)
            qq = jnp.concatenate([jnp.where(first_q, qp, zero), jnp.where(first_q, zero, qp)], axis=0)
            return lax.dot_general(qq, kw_ref[wrows(u), cols(p)], (((1,), (1,)), ((), ())),
                                   preferred_element_type=F32)

        def softmax(u, p, s):
            s = s + bias_ref[p]
            if first_block:
                s = s + jnp.where(win_col + u * ATTN_QROWS >= rows, 0.0, NEG_BIG)
            m = jnp.max(s, axis=-1, keepdims=True)
            e = jnp.exp2(s - m)
            return e.astype(BF16), jnp.sum(e, axis=-1, keepdims=True)

        def emit(u, p, prob, denom):
            o = jnp.dot(prob, vw_ref[wrows(u), cols(p)], preferred_element_type=F32) / denom
            out_ref[qrows(u), cols(p)] = jnp.where(
                first_q, o[:ATTN_QROWS], o[ATTN_QROWS:]).astype(BF16)

        scores, soft = {}, {}
        for step in range(len(units) + 3):
            if step < len(units):
                scores[step] = score(*units[step])
            if 0 <= step - 2 < len(units):
                soft[step - 2] = softmax(*units[step - 2], scores.pop(step - 2))
            if 0 <= step - 3 < len(units):
                emit(*units[step - 3], *soft.pop(step - 3))

    @pl.when(n == 0)
    def _():
        run(True)

    @pl.when(n > 0)
    def _():
        run(False)


def _band_bias(rel_bias):
    rb = rel_bias.astype(F32)
    n_far = (CHUNK - 1) + (B_PAD - MAX_REL) + 1
    far = jnp.broadcast_to(rb[:, 2 * MAX_REL:], (B_HEADS, n_far))
    near = rb[:, MAX_REL - (CHUNK - 1):2 * MAX_REL][:, ::-1]
    tb = jnp.concatenate([far, near], axis=1)
    band = jnp.stack([tb[:, CHUNK - 1 - c:CHUNK - 1 - c + B_BAND] for c in range(CHUNK)], axis=1)
    lo = jnp.pad(band, ((0, 0), (0, 0), (0, CHUNK)), constant_values=NEG_BIG)
    hi = jnp.pad(band, ((0, 0), (0, 0), (CHUNK, 0)), constant_values=NEG_BIG)
    unit = jnp.concatenate([lo, hi], axis=1)
    return (unit * LOG2E).reshape(B_HEADS // 2, 2 * ATTN_QROWS, ATTN_WIN)


def _attn(z, bias):
    rows = ATTN_ROWS
    nb = SEQ // rows

    def cur(cb):
        return pl.BlockSpec((rows, COL_W), lambda b, n: (b * nb + n, cb))

    def prev(cb):
        return pl.BlockSpec((rows, COL_W), lambda b, n: (b * nb + jnp.maximum(n - 1, 0), cb))

    return pl.pallas_call(
        _attn_kernel,
        grid=(BATCH, nb),
        in_specs=[
            cur(Z_QB), cur(Z_KB), prev(Z_VB), cur(Z_VB),
            _resident((B_HEADS // 2, 2 * ATTN_QROWS, ATTN_WIN)),
        ],
        out_specs=pl.BlockSpec((rows, B_WIDTH), lambda b, n: (b * nb + n, 0)),
        out_shape=jax.ShapeDtypeStruct((TOKENS, B_WIDTH), BF16),
        scratch_shapes=[
            pltpu.VMEM((2 * rows, B_WIDTH), BF16),
            pltpu.VMEM((2 * rows, B_WIDTH), BF16),
        ],
        compiler_params=_params("arbitrary", "arbitrary"),
        name="band_attn",
    )(z, z, z, z, bias)


def _merge_kernel(x_ref, oa_ref, ob_ref, ga0_ref, ga1_ref, gb0_ref, gb1_ref,
                  wa_ref, wb_ref, wo_ref, out_ref):
    ga = jnp.concatenate([ga0_ref[...], ga1_ref[...]], axis=1).astype(F32)
    gb = jnp.concatenate([gb0_ref[...], gb1_ref[...]], axis=1).astype(F32)
    pa = jnp.dot(oa_ref[...], wa_ref[...], preferred_element_type=F32)
    pb = jnp.dot(ob_ref[...], wb_ref[...], preferred_element_type=F32)
    merged = (_sigmoid(ga) * pa + _sigmoid(gb) * pb).astype(BF16)
    out_ref[...] = x_ref[...] + jnp.dot(merged, wo_ref[...], preferred_element_type=F32)


def _merge(x, oa, ob, z, wa, wb, wo):
    tm = MERGE_TM

    def zcol(cb):
        return pl.BlockSpec((tm, COL_W), lambda i: (i, cb))

    return pl.pallas_call(
        _merge_kernel,
        grid=(TOKENS // tm,),
        in_specs=[
            pl.BlockSpec((tm, D_MODEL), lambda i: (i, 0)),
            pl.BlockSpec((tm, A_WIDTH), lambda i: (i, 0)),
            pl.BlockSpec((tm, B_WIDTH), lambda i: (i, 0)),
            zcol(Z_GATE_A), zcol(Z_GATE_A + 1), zcol(Z_GATE_B), zcol(Z_GATE_B + 1),
            _resident((A_WIDTH, D_MODEL)),
            _resident((B_WIDTH, D_MODEL)),
            _resident((D_MODEL, D_MODEL)),
        ],
        out_specs=pl.BlockSpec((tm, D_MODEL), lambda i: (i, 0)),
        out_shape=jax.ShapeDtypeStruct((TOKENS, D_MODEL), F32),
        compiler_params=_params("arbitrary"),
        name="merge_out",
    )(x, oa, ob, z, z, z, z, wa, wb, wo)


def _ffn_kernel(x_ref, g_ref, wg_ref, wu_ref, wd_ref, out_ref):
    x = x_ref[...]
    h = _rmsnorm_bf16(x, g_ref[...])
    gate = jnp.dot(h, wg_ref[...], preferred_element_type=F32)
    up = jnp.dot(h, wu_ref[...], preferred_element_type=F32)
    act = (gate * _sigmoid(gate) * up).astype(BF16)
    out_ref[...] = x + jnp.dot(act, wd_ref[...], preferred_element_type=F32)


def _ffn(x, g, wg, wu, wd):
    tm = FFN_TM
    return pl.pallas_call(
        _ffn_kernel,
        grid=(TOKENS // tm,),
        in_specs=[
            pl.BlockSpec((tm, D_MODEL), lambda i: (i, 0)),
            _resident((1, D_MODEL)),
            _resident((D_MODEL, D_FF)),
            _resident((D_MODEL, D_FF)),
            _resident((D_FF, D_MODEL)),
        ],
        out_specs=pl.BlockSpec((tm, D_MODEL), lambda i: (i, 0)),
        out_shape=jax.ShapeDtypeStruct((TOKENS, D_MODEL), F32),
        compiler_params=_params("arbitrary"),
        name="ffn",
    )(x, g, wg, wu, wd)


def _ple_kernel(x_ref, p_ref, g_ref, wg_ref, wp_ref, out_ref):
    x = x_ref[...]
    h = _rmsnorm_bf16(x, g_ref[...])
    gate = _sigmoid(jnp.dot(h, wg_ref[...], preferred_element_type=F32))
    emb = jnp.dot(p_ref[...].astype(BF16), wp_ref[...], preferred_element_type=F32)
    out_ref[...] = x + gate * emb


def _ple(li, x, p, g, wg, wp):
    tm = PLE_TM
    return pl.pallas_call(
        _ple_kernel,
        grid=(TOKENS // tm,),
        in_specs=[
            pl.BlockSpec((tm, D_MODEL), lambda i: (i, 0)),
            pl.BlockSpec((None, tm, PLE_DIM), lambda i: (li, i, 0)),
            _resident((1, D_MODEL)),
            _resident((D_MODEL, D_MODEL)),
            _resident((PLE_DIM, D_MODEL)),
        ],
        out_specs=pl.BlockSpec((tm, D_MODEL), lambda i: (i, 0)),
        out_shape=jax.ShapeDtypeStruct((TOKENS, D_MODEL), F32),
        compiler_params=_params("arbitrary"),
        name="ple",
    )(x, p, g, wg, wp)


def kernel(x, p, norm_mix_g, w_in, hgrn_lb_logits, hgrn_onorm_g, attn_qnorm_g, attn_knorm_g,
           attn_rel_bias, w_branch_a, w_branch_b, w_out, norm_ffn_g, w_ffn_gate, w_ffn_up,
           w_ffn_down, norm_ple_g, w_ple_gate, w_ple_proj):
    xt = x.reshape(TOKENS, D_MODEL)
    pt = p.reshape(DEPTH, TOKENS, PLE_DIM)
    for li in range(DEPTH):
        qg2 = jnp.concatenate([attn_qnorm_g[li], attn_qnorm_g[li]]).reshape(1, LANES)
        kg2 = jnp.concatenate([attn_knorm_g[li], attn_knorm_g[li]]).reshape(1, LANES)
        z, logf = _inproj(li, xt, norm_mix_g[li].reshape(1, D_MODEL), w_in[li].astype(BF16),
                          hgrn_lb_logits, qg2, kg2)
        oa = _hgrn(z, logf, hgrn_onorm_g[li].reshape(1, A_DV))
        ob = _attn(z, _band_bias(attn_rel_bias[li]))
        xt = _merge(xt, oa, ob, z, w_branch_a[li].astype(BF16), w_branch_b[li].astype(BF16),
                    w_out[li].astype(BF16))
        xt = _ffn(xt, norm_ffn_g[li].reshape(1, D_MODEL), w_ffn_gate[li].astype(BF16),
                  w_ffn_up[li].astype(BF16), w_ffn_down[li].astype(BF16))
        xt = _ple(li, xt, pt, norm_ple_g[li].reshape(1, D_MODEL), w_ple_gate[li].astype(BF16),
                  w_ple_proj[li].astype(BF16))
    return xt.reshape(BATCH, SEQ, D_MODEL)
```

```python
import functools
import math

import jax
import jax.numpy as jnp
from jax import lax
from jax.experimental import pallas as pl
from jax.experimental.pallas import tpu as pltpu

F32 = jnp.float32
BF16 = jnp.bfloat16

D_MODEL = 1024
BATCH = 8
SEQ = 4096
DEPTH = 4
TOKENS = BATCH * SEQ
CHUNK = 64
PLE_DIM = 256
EPS = 1e-6
A_HEADS = 4
A_DK = 128
A_DV = 128
A_WIDTH = A_HEADS * A_DV
B_HEADS = 8
B_HD = 64
B_WIDTH = B_HEADS * B_HD
B_LEFT_CHUNKS = 8
B_PAD = B_LEFT_CHUNKS * CHUNK
B_BAND = B_PAD + CHUNK
MAX_REL = 128
D_FF = 2816
D_IN = 5632
NEG_BIG = -1e30
LOG2E = math.log2(math.e)
ATTN_Q_SCALE = B_HD ** -0.5 * LOG2E

COL_W = 512
W_QA, W_FA, W_IA, W_GA, W_QB, W_KB, W_VB, W_GATE_A, W_GATE_B = 0, 1, 2, 3, 4, 5, 6, 7, 9
N_W_COLS = D_IN // COL_W
Z_OF_W = {0: 0, 2: 1, 3: 2, 4: 3, 5: 4, 6: 5, 7: 6, 8: 7, 9: 8, 10: 9}
Z_QA, Z_IA, Z_GA, Z_QB, Z_KB, Z_VB, Z_GATE_A, Z_GATE_B = 0, 1, 2, 3, 4, 5, 6, 8
Z_WIDTH = D_IN - COL_W

SUBLANES = 8
LANES = 128
VMEM_LIMIT_BYTES = 56 * 1024 * 1024

INPROJ_TM = 512
HGRN_ROWS = 1024
HGRN_NC = 4
ATTN_ROWS = B_PAD
ATTN_QROWS = 2 * CHUNK
ATTN_WIN = B_PAD + ATTN_QROWS
POST_TM = 512
SUB = SUBLANES
N_SUB = CHUNK // SUB
BLK = 16
N_BLK = CHUNK // BLK
SAFE_BLOCK_NATS = 64.0


def _params(*sem):
    return pltpu.CompilerParams(dimension_semantics=sem, vmem_limit_bytes=VMEM_LIMIT_BYTES)


def _resident(shape):
    nd = len(shape)
    return pl.BlockSpec(shape, lambda *_: (0,) * nd, pipeline_mode=pl.Buffered(1))


def _rmsnorm_bf16(x, g):
    ms = jnp.mean(x * x, axis=-1, keepdims=True)
    return (x * lax.rsqrt(ms + EPS) * g).astype(BF16)


def _sigmoid(x):
    return 1.0 / (1.0 + jnp.exp(-x))


def _pair_head_rms(t, g_pair):
    first = lax.broadcasted_iota(jnp.int32, t.shape, 1) < B_HD
    sq = t * t
    s0 = jnp.sum(jnp.where(first, sq, 0.0), axis=-1, keepdims=True)
    s1 = jnp.sum(jnp.where(first, 0.0, sq), axis=-1, keepdims=True)
    ms = jnp.where(first, s0, s1) * (1.0 / B_HD)
    return t * lax.rsqrt(ms + EPS) * g_pair


def _lower_bound(li, lbl):
    e = jnp.exp(lbl - jnp.max(lbl, axis=0, keepdims=True))
    soft = e / jnp.sum(e, axis=0, keepdims=True)
    lb = jnp.zeros((1, COL_W), F32)
    for r in range(1, li + 1):
        lb = lb + soft[r:r + 1, :]
    return lb


def _log_forget(z, lb):
    t = jnp.exp(-jnp.abs(z))
    num = jnp.where(z >= 0.0, 1.0 + lb * t, lb + t)
    return jnp.where(num > 0.0, jnp.log(num), z) - jnp.log(1.0 + t)


def _inproj_kernel(li, x_ref, g_ref, w_ref, lbl_ref, qg_ref, kg_ref, z_ref, logf_ref):
    h = _rmsnorm_bf16(x_ref[...], g_ref[...])
    lb = _lower_bound(li, lbl_ref[...])
    n_slices = 4
    slice_rows = INPROJ_TM // n_slices

    def heads_normed(acc, g_pair, scale):
        tiles = [_pair_head_rms(acc[:, p * LANES:(p + 1) * LANES], g_pair) * scale
                 for p in range(COL_W // LANES)]
        return jnp.concatenate(tiles, axis=1)

    def finish(cb, acc):
        if cb == W_FA:
            logf_ref[...] = acc
            return
        if cb == W_GA:
            acc = acc * _sigmoid(acc)
        elif cb == W_QB:
            acc = heads_normed(acc, qg_ref[...], ATTN_Q_SCALE)
        elif cb == W_KB:
            acc = heads_normed(acc, kg_ref[...], 1.0)
        zc = Z_OF_W[cb]
        z_ref[:, zc * COL_W:(zc + 1) * COL_W] = acc.astype(BF16)

    def forget_slice(s):
        rows = slice(s * slice_rows, (s + 1) * slice_rows)
        logf_ref[rows, :] = _log_forget(logf_ref[rows, :], lb)

    pending = None
    for cb in range(N_W_COLS):
        acc = jnp.dot(h, w_ref[:, cb * COL_W:(cb + 1) * COL_W], preferred_element_type=F32)
        if pending is not None:
            finish(*pending)
        pending = (cb, acc)
        s = cb - (W_FA + 2)
        if 0 <= s < n_slices:
            forget_slice(s)
    finish(*pending)


def _inproj(li, x, g, w, lb_logits, qg2, kg2):
    tm = INPROJ_TM
    return pl.pallas_call(
        functools.partial(_inproj_kernel, li),
        grid=(TOKENS // tm,),
        in_specs=[
            pl.BlockSpec((tm, D_MODEL), lambda i: (i, 0)),
            _resident((1, D_MODEL)),
            _resident((D_MODEL, D_IN)),
            _resident((DEPTH, COL_W)),
            _resident((1, LANES)),
            _resident((1, LANES)),
        ],
        out_specs=[
            pl.BlockSpec((tm, Z_WIDTH), lambda i: (i, 0)),
            pl.BlockSpec((tm, COL_W), lambda i: (i, 0)),
        ],
        out_shape=[
            jax.ShapeDtypeStruct((TOKENS, Z_WIDTH), BF16),
            jax.ShapeDtypeStruct((TOKENS, COL_W), F32),
        ],
        compiler_params=_params("arbitrary"),
        name="inproj",
    )(x, g, w, lb_logits, qg2, kg2)


def _split3_bf16(v):
    hi = v.astype(BF16)
    r1 = v - hi.astype(F32)
    mid = r1.astype(BF16)
    lo = (r1 - mid.astype(F32)).astype(BF16)
    return hi, mid, lo


def _chunk_cumsum(lf):
    row = lax.broadcasted_iota(jnp.int32, (CHUNK, CHUNK), 0)
    col = lax.broadcasted_iota(jnp.int32, (CHUNK, CHUNK), 1)
    tril = jnp.where(row >= col, 1.0, 0.0).astype(BF16)
    hi, mid, lo = _split3_bf16(lf)
    return (jnp.dot(tril, hi, preferred_element_type=F32)
            + jnp.dot(tril, mid, preferred_element_type=F32)
            + jnp.dot(tril, lo, preferred_element_type=F32))


def _hgrn_rows_fast(q_ref, lf_ref, i_ref, gs_ref, og_ref, out_ref, st_ref, r0):
    nrow = HGRN_NC * CHUNK
    rows = pl.ds(r0, nrow)
    lf = lf_ref[rows, :]
    row = lax.broadcasted_iota(jnp.int32, (nrow, nrow), 0)
    col = lax.broadcasted_iota(jnp.int32, (nrow, nrow), 1)
    shift = int(math.log2(CHUNK))
    same_chunk = lax.shift_right_logical(row, shift) == lax.shift_right_logical(col, shift)
    tril = jnp.where((row >= col) & same_chunk, 1.0, 0.0).astype(BF16)
    hi, mid, lo = _split3_bf16(lf)
    b = (jnp.dot(tril, hi, preferred_element_type=F32)
         + jnp.dot(tril, mid, preferred_element_type=F32)
         + jnp.dot(tril, lo, preferred_element_type=F32)) * LOG2E
    k = 1.0 - jnp.exp(lf)
    qs = q_ref[rows, :].astype(F32) * (A_DK ** -0.5)
    q_dec = (qs * jnp.exp2(b)).astype(BF16)

    zero_row = jnp.zeros((1, A_WIDTH), F32)
    c_rows = [[zero_row if j == 0 else b[c * CHUNK + BLK * j - 1:c * CHUNK + BLK * j, :]
               for j in range(N_BLK)] for c in range(HGRN_NC)]
    cref = jnp.concatenate([jnp.broadcast_to(c_rows[c][j], (BLK, A_WIDTH))
                            for c in range(HGRN_NC) for j in range(N_BLK)], axis=0)
    q_all = (qs * jnp.exp2(b - cref)).astype(BF16)
    b_last = [b[(c + 1) * CHUNK - 1:(c + 1) * CHUNK, :] for c in range(HGRN_NC)]
    k_dec = jnp.concatenate(
        [k[c * CHUNK:(c + 1) * CHUNK, :] * jnp.exp2(b_last[c] - b[c * CHUNK:(c + 1) * CHUNK, :])
         for c in range(HGRN_NC)], axis=0).astype(BF16)
    k_blk = [[(k[c * CHUNK:c * CHUNK + BLK * (j + 1), :]
               * jnp.exp2(c_rows[c][j] - b[c * CHUNK:c * CHUNK + BLK * (j + 1), :])).astype(BF16)
              for j in range(N_BLK)] for c in range(HGRN_NC)]
    iv = i_ref[rows, :]
    gs = gs_ref[rows, :]
    causal = (lax.broadcasted_iota(jnp.int32, (CHUNK, CHUNK), 0)
              >= lax.broadcasted_iota(jnp.int32, (CHUNK, CHUNK), 1))

    def zeros16(n):
        return [jnp.zeros((BLK, A_DK), BF16)] * n

    units = [(h, c) for h in range(A_HEADS) for c in range(HGRN_NC)]

    def hcols(h):
        return slice(h * A_DK, (h + 1) * A_DK)

    def crows(c):
        return slice(c * CHUNK, (c + 1) * CHUNK)

    attn, upd = {}, {}
    for h, c in units:
        hc = hcols(h)
        q_cat = jnp.concatenate(
            [jnp.concatenate(zeros16(j) + [q_all[c * CHUNK + BLK * j:c * CHUNK + BLK * (j + 1), hc]]
                             + zeros16(N_BLK - 1 - j), axis=0) for j in range(N_BLK)], axis=1)
        k_cat = jnp.concatenate(
            [jnp.concatenate([k_blk[c][j][:, hc]] + zeros16(N_BLK - 1 - j), axis=0)
             for j in range(N_BLK)], axis=1)
        attn[h, c] = lax.dot_general(q_cat, k_cat, (((1,), (1,)), ((), ())),
                                     preferred_element_type=F32)
    for h, c in units:
        upd[h, c] = lax.dot_general(iv[crows(c), hcols(h)], k_dec[crows(c), hcols(h)],
                                    (((0,), (0,)), ((), ())), preferred_element_type=F32)
    o_state = {}
    for h in range(A_HEADS):
        st = st_ref[h]
        for c in range(HGRN_NC):
            o_state[h, c] = lax.dot_general(q_dec[crows(c), hcols(h)], st.astype(BF16),
                                            (((1,), (1,)), ((), ())), preferred_element_type=F32)
            st = st * jnp.exp2(b_last[c][:, hcols(h)]) + upd[h, c]
        st_ref[h] = st
    for h, c in units:
        a = jnp.where(causal, attn[h, c], 0.0).astype(BF16)
        o = jnp.dot(a, iv[crows(c), hcols(h)], preferred_element_type=F32) + o_state[h, c]
        ms = jnp.mean(o * o, axis=-1, keepdims=True)
        y = o * lax.rsqrt(ms + EPS) * og_ref[...] * gs[crows(c), hcols(h)].astype(F32)
        out_ref[pl.ds(r0 + c * CHUNK, CHUNK), hcols(h)] = y.astype(BF16)


def _hgrn_head_chunk_exact(qs, lf, iv, st):
    b = _chunk_cumsum(lf)
    k = 1.0 - jnp.exp(lf)
    b_last = b[CHUNK - 1:CHUNK, :]

    o = lax.dot_general((qs * jnp.exp(b)).astype(BF16), st.astype(BF16),
                        (((1,), (1,)), ((), ())), preferred_element_type=F32)

    zero_blk = jnp.zeros((SUB, A_DK), F32)
    c_rows = [None] + [b[SUB * i - 1:SUB * i, :] for i in range(1, N_SUB)]
    cref = jnp.concatenate(
        [zero_blk] + [jnp.broadcast_to(c_rows[i], (SUB, A_DK)) for i in range(1, N_SUB)], axis=0)
    q_all = qs * jnp.exp(b - cref)
    q_parts, k_parts = [], []
    for i in range(1, N_SUB):
        q_parts.append(jnp.concatenate(
            [zero_blk] * i + [q_all[SUB * i:SUB * (i + 1), :]] + [zero_blk] * (N_SUB - 1 - i), axis=0))
        kd = k[:SUB * i, :] * jnp.exp(c_rows[i] - b[:SUB * i, :])
        k_parts.append(jnp.concatenate([kd, jnp.zeros((CHUNK - SUB * i, A_DK), F32)], axis=0))
    q_cat = jnp.concatenate(q_parts, axis=1).astype(BF16)
    k_cat = jnp.concatenate(k_parts, axis=1).astype(BF16)
    attn = lax.dot_general(q_cat, k_cat, (((1,), (1,)), ((), ())),
                           preferred_element_type=F32)

    t_loc = lax.broadcasted_iota(jnp.int32, (SUB, CHUNK), 0)
    lane = lax.broadcasted_iota(jnp.int32, (SUB, CHUNK), 1)
    diag_blocks = []
    for i in range(N_SUB):
        qi = qs[SUB * i:SUB * (i + 1), :]
        bi = b[SUB * i:SUB * (i + 1), :]
        blk = jnp.zeros((SUB, CHUNK), F32)
        for jj in range(SUB):
            s = SUB * i + jj
            p = qi * k[s:s + 1, :] * jnp.exp(bi - b[s:s + 1, :])
            colsum = jnp.sum(p, axis=-1, keepdims=True)
            blk = jnp.where((lane == s) & (t_loc >= jj), colsum, blk)
        diag_blocks.append(blk)
    attn = attn + jnp.concatenate(diag_blocks, axis=0)

    o = o + jnp.dot(attn.astype(BF16), iv, preferred_element_type=F32)

    k_dec = (k * jnp.exp(b_last - b)).astype(BF16)
    st_new = st * jnp.exp(b_last) + lax.dot_general(
        iv, k_dec, (((0,), (0,)), ((), ())), preferred_element_type=F32)
    return o, st_new


def _hgrn_kernel(q_ref, lf_ref, i_ref, gs_ref, og_ref, out_ref, st_ref):
    @pl.when(pl.program_id(1) == 0)
    def _():
        st_ref[...] = jnp.zeros_like(st_ref)

    n_blocks = HGRN_ROWS // BLK
    blk_of_row = lax.shift_right_logical(
        lax.broadcasted_iota(jnp.int32, (n_blocks, HGRN_ROWS), 1), int(math.log2(BLK)))
    ind = jnp.where(blk_of_row == lax.broadcasted_iota(jnp.int32, (n_blocks, HGRN_ROWS), 0),
                    1.0, 0.0).astype(BF16)
    blk_sums = jnp.dot(ind, lf_ref[...].astype(BF16), preferred_element_type=F32)
    safe = jnp.min(blk_sums) >= -SAFE_BLOCK_NATS

    @pl.when(safe)
    def _():
        def rows_body(it, carry):
            r0 = pl.multiple_of(it * (HGRN_NC * CHUNK), HGRN_NC * CHUNK)
            _hgrn_rows_fast(q_ref, lf_ref, i_ref, gs_ref, og_ref, out_ref, st_ref, r0)
            return carry

        lax.fori_loop(0, HGRN_ROWS // (HGRN_NC * CHUNK), rows_body, 0)

    @pl.when(jnp.logical_not(safe))
    def _():
        def chunk_body(c, carry):
            r0 = pl.multiple_of(c * CHUNK, CHUNK)
            rows = pl.ds(r0, CHUNK)
            for h in range(A_HEADS):
                cols = slice(h * A_DK, (h + 1) * A_DK)
                qs = q_ref[rows, cols].astype(F32) * (A_DK ** -0.5)
                o, st_new = _hgrn_head_chunk_exact(qs, lf_ref[rows, cols], i_ref[rows, cols], st_ref[h])
                st_ref[h] = st_new
                ms = jnp.mean(o * o, axis=-1, keepdims=True)
                y = o * lax.rsqrt(ms + EPS) * og_ref[...] * gs_ref[rows, cols].astype(F32)
                out_ref[rows, cols] = y.astype(BF16)
            return carry

        lax.fori_loop(0, HGRN_ROWS // CHUNK, chunk_body, 0)


def _hgrn(z, logf, onorm_g):
    nb = SEQ // HGRN_ROWS

    def zcol(cb):
        return pl.BlockSpec((HGRN_ROWS, COL_W), lambda b, n: (b * nb + n, cb))

    return pl.pallas_call(
        _hgrn_kernel,
        grid=(BATCH, nb),
        in_specs=[
            zcol(Z_QA),
            pl.BlockSpec((HGRN_ROWS, COL_W), lambda b, n: (b * nb + n, 0)),
            zcol(Z_IA),
            zcol(Z_GA),
            _resident((1, A_DV)),
        ],
        out_specs=pl.BlockSpec((HGRN_ROWS, A_WIDTH), lambda b, n: (b * nb + n, 0)),
        out_shape=jax.ShapeDtypeStruct((TOKENS, A_WIDTH), BF16),
        scratch_shapes=[pltpu.VMEM((A_HEADS, A_DV, A_DK), F32)],
        compiler_params=_params("arbitrary", "arbitrary"),
        name="hgrn2",
    )(z, logf, z, z, onorm_g)


def _attn_kernel(q_ref, kp_ref, kc_ref, vp_ref, vc_ref, bias_ref, out_ref):
    n = pl.program_id(1)
    rows = ATTN_ROWS
    first_q = lax.broadcasted_iota(jnp.int32, (ATTN_QROWS, LANES), 1) < B_HD

    def run(first_block):
        units = [(u, p) for u in range(rows // ATTN_QROWS) for p in range(B_HEADS // 2)]

        def qrows(u):
            return slice(u * ATTN_QROWS, (u + 1) * ATTN_QROWS)

        def prev_rows(u):
            return slice(u * ATTN_QROWS, rows)

        def cur_rows(u):
            return slice(0, (u + 1) * ATTN_QROWS)

        def cols(p):
            return slice(p * LANES, (p + 1) * LANES)

        def key_dot(a, b_ref, r, p):
            return lax.dot_general(a, b_ref[r, cols(p)], (((1,), (1,)), ((), ())),
                                   preferred_element_type=F32)

        def score(u, p):
            qp = q_ref[qrows(u), cols(p)]
            zero = jnp.zeros_like(qp)
            qq = jnp.concatenate([jnp.where(first_q, qp, zero), jnp.where(first_q, zero, qp)], axis=0)
            s_cur = key_dot(qq, kc_ref, cur_rows(u), p)
            if first_block:
                return s_cur
            return jnp.concatenate([key_dot(qq, kp_ref, prev_rows(u), p), s_cur], axis=1)

        def softmax(u, p, s):
            n_prev = 0 if first_block else rows - u * ATTN_QROWS
            s = s + bias_ref[p, :, ATTN_WIN - (n_prev + (u + 1) * ATTN_QROWS):]
            m = jnp.max(s, axis=-1, keepdims=True)
            e = jnp.exp2(s - m)
            return e.astype(BF16), jnp.sum(e, axis=-1, keepdims=True)

        def emit(u, p, prob, denom):
            n_prev = 0 if first_block else rows - u * ATTN_QROWS
            o = jnp.dot(prob[:, n_prev:], vc_ref[cur_rows(u), cols(p)], preferred_element_type=F32)
            if not first_block:
                o = o + jnp.dot(prob[:, :n_prev], vp_ref[prev_rows(u), cols(p)],
                                preferred_element_type=F32)
            o = o / denom
            out_ref[qrows(u), cols(p)] = jnp.where(
                first_q, o[:ATTN_QROWS], o[ATTN_QROWS:]).astype(BF16)

        scores, soft = {}, {}
        for step in range(len(units) + 3):
            if step < len(units):
                scores[step] = score(*units[step])
            if 0 <= step - 2 < len(units):
                soft[step - 2] = softmax(*units[step - 2], scores.pop(step - 2))
            if 0 <= step - 3 < len(units):
                emit(*units[step - 3], *soft.pop(step - 3))

    @pl.when(n == 0)
    def _():
        run(True)

    @pl.when(n > 0)
    def _():
        run(False)


def _band_bias(rel_bias):
    rb = rel_bias.astype(F32)
    n_far = (CHUNK - 1) + (B_PAD - MAX_REL) + 1
    far = jnp.broadcast_to(rb[:, 2 * MAX_REL:], (B_HEADS, n_far))
    near = rb[:, MAX_REL - (CHUNK - 1):2 * MAX_REL][:, ::-1]
    tb = jnp.concatenate([far, near], axis=1)
    band = jnp.stack([tb[:, CHUNK - 1 - c:CHUNK - 1 - c + B_BAND] for c in range(CHUNK)], axis=1)
    lo = jnp.pad(band, ((0, 0), (0, 0), (0, CHUNK)), constant_values=NEG_BIG)
    hi = jnp.pad(band, ((0, 0), (0, 0), (CHUNK, 0)), constant_values=NEG_BIG)
    unit = jnp.concatenate([lo, hi], axis=1)
    return (unit * LOG2E).reshape(B_HEADS // 2, 2 * ATTN_QROWS, ATTN_WIN)


def _attn(z, bias):
    rows = ATTN_ROWS
    nb = SEQ // rows

    def cur(cb):
        return pl.BlockSpec((rows, COL_W), lambda b, n: (b * nb + n, cb))

    def prev(cb):
        return pl.BlockSpec((rows, COL_W), lambda b, n: (b * nb + jnp.maximum(n - 1, 0), cb))

    return pl.pallas_call(
        _attn_kernel,
        grid=(BATCH, nb),
        in_specs=[
            cur(Z_QB), prev(Z_KB), cur(Z_KB), prev(Z_VB), cur(Z_VB),
            _resident((B_HEADS // 2, 2 * ATTN_QROWS, ATTN_WIN)),
        ],
        out_specs=pl.BlockSpec((rows, B_WIDTH), lambda b, n: (b * nb + n, 0)),
        out_shape=jax.ShapeDtypeStruct((TOKENS, B_WIDTH), BF16),
        compiler_params=_params("arbitrary", "arbitrary"),
        name="band_attn",
    )(z, z, z, z, z, bias)


def _post_kernel(x_ref, oa_ref, ob_ref, ga0_ref, ga1_ref, gb0_ref, gb1_ref, p_ref,
                 wa_ref, wb_ref, wo_ref, gf_ref, wg_ref, wu_ref, wd_ref, gp_ref, wpg_ref, wpp_ref,
                 out_ref):
    ga = jnp.concatenate([ga0_ref[...], ga1_ref[...]], axis=1).astype(F32)
    gb = jnp.concatenate([gb0_ref[...], gb1_ref[...]], axis=1).astype(F32)
    pa = jnp.dot(oa_ref[...], wa_ref[...], preferred_element_type=F32)
    pb = jnp.dot(ob_ref[...], wb_ref[...], preferred_element_type=F32)
    merged = (_sigmoid(ga) * pa + _sigmoid(gb) * pb).astype(BF16)
    x1 = x_ref[...] + jnp.dot(merged, wo_ref[...], preferred_element_type=F32)
    h = _rmsnorm_bf16(x1, gf_ref[...])
    gate = jnp.dot(h, wg_ref[...], preferred_element_type=F32)
    up = jnp.dot(h, wu_ref[...], preferred_element_type=F32)
    act = (gate * _sigmoid(gate) * up).astype(BF16)
    x2 = x1 + jnp.dot(act, wd_ref[...], preferred_element_type=F32)
    emb = jnp.dot(p_ref[...].astype(BF16), wpp_ref[...], preferred_element_type=F32)
    h2 = _rmsnorm_bf16(x2, gp_ref[...])
    pgate = _sigmoid(jnp.dot(h2, wpg_ref[...], preferred_element_type=F32))
    out_ref[...] = x2 + pgate * emb


def _post(li, x, oa, ob, z, p, wa, wb, wo, gf, wg, wu, wd, gp, wpg, wpp):
    tm = POST_TM

    def zcol(cb):
        return pl.BlockSpec((tm, COL_W), lambda i: (i, cb))

    return pl.pallas_call(
        _post_kernel,
        grid=(TOKENS // tm,),
        in_specs=[
            pl.BlockSpec((tm, D_MODEL), lambda i: (i, 0)),
            pl.BlockSpec((tm, A_WIDTH), lambda i: (i, 0)),
            pl.BlockSpec((tm, B_WIDTH), lambda i: (i, 0)),
            zcol(Z_GATE_A), zcol(Z_GATE_A + 1), zcol(Z_GATE_B), zcol(Z_GATE_B + 1),
            pl.BlockSpec((None, tm, PLE_DIM), lambda i: (li, i, 0)),
            _resident((A_WIDTH, D_MODEL)),
            _resident((B_WIDTH, D_MODEL)),
            _resident((D_MODEL, D_MODEL)),
            _resident((1, D_MODEL)),
            _resident((D_MODEL, D_FF)),
            _resident((D_MODEL, D_FF)),
            _resident((D_FF, D_MODEL)),
            _resident((1, D_MODEL)),
            _resident((D_MODEL, D_MODEL)),
            _resident((PLE_DIM, D_MODEL)),
        ],
        out_specs=pl.BlockSpec((tm, D_MODEL), lambda i: (i, 0)),
        out_shape=jax.ShapeDtypeStruct((TOKENS, D_MODEL), F32),
        compiler_params=_params("arbitrary"),
        name="post_mix",
    )(x, oa, ob, z, z, z, z, p, wa, wb, wo, gf, wg, wu, wd, gp, wpg, wpp)


def kernel(x, p, norm_mix_g, w_in, hgrn_lb_logits, hgrn_onorm_g, attn_qnorm_g, attn_knorm_g,
           attn_rel_bias, w_branch_a, w_branch_b, w_out, norm_ffn_g, w_ffn_gate, w_ffn_up,
           w_ffn_down, norm_ple_g, w_ple_gate, w_ple_proj):
    xt = x.reshape(TOKENS, D_MODEL)
    pt = p.reshape(DEPTH, TOKENS, PLE_DIM)
    for li in range(DEPTH):
        qg2 = jnp.concatenate([attn_qnorm_g[li], attn_qnorm_g[li]]).reshape(1, LANES)
        kg2 = jnp.concatenate([attn_knorm_g[li], attn_knorm_g[li]]).reshape(1, LANES)
        z, logf = _inproj(li, xt, norm_mix_g[li].reshape(1, D_MODEL), w_in[li].astype(BF16),
                          hgrn_lb_logits, qg2, kg2)
        oa = _hgrn(z, logf, hgrn_onorm_g[li].reshape(1, A_DV))
        ob = _attn(z, _band_bias(attn_rel_bias[li]))
        xt = _post(li, xt, oa, ob, z, pt,
                   w_branch_a[li].astype(BF16), w_branch_b[li].astype(BF16), w_out[li].astype(BF16),
                   norm_ffn_g[li].reshape(1, D_MODEL), w_ffn_gate[li].astype(BF16),
                   w_ffn_up[li].astype(BF16), w_ffn_down[li].astype(BF16),
                   norm_ple_g[li].reshape(1, D_MODEL), w_ple_gate[li].astype(BF16),
                   w_ple_proj[li].astype(BF16))
    return xt.reshape(BATCH, SEQ, D_MODEL)
```

```python
import functools
import math

import jax
import jax.numpy as jnp
from jax import lax
from jax.experimental import pallas as pl
from jax.experimental.pallas import tpu as pltpu

F32 = jnp.float32
BF16 = jnp.bfloat16

D_MODEL = 1024
BATCH = 8
SEQ = 4096
DEPTH = 4
TOKENS = BATCH * SEQ
CHUNK = 64
PLE_DIM = 256
EPS = 1e-6
A_HEADS = 4
A_DK = 128
A_DV = 128
A_WIDTH = A_HEADS * A_DV
B_HEADS = 8
B_HD = 64
B_WIDTH = B_HEADS * B_HD
B_LEFT_CHUNKS = 8
B_PAD = B_LEFT_CHUNKS * CHUNK
B_BAND = B_PAD + CHUNK
MAX_REL = 128
D_FF = 2816
D_IN = 5632
NEG_BIG = -1e30
LOG2E = math.log2(math.e)
ATTN_Q_SCALE = B_HD ** -0.5 * LOG2E

COL_W = 512
W_QA, W_FA, W_IA, W_GA, W_QB, W_KB, W_VB, W_GATE_A, W_GATE_B = 0, 1, 2, 3, 4, 5, 6, 7, 9
N_W_COLS = D_IN // COL_W
Z_OF_W = {0: 0, 2: 1, 3: 2, 4: 3, 5: 4, 6: 5, 7: 6, 8: 7, 9: 8, 10: 9}
Z_QA, Z_IA, Z_GA, Z_QB, Z_KB, Z_VB, Z_GATE_A, Z_GATE_B = 0, 1, 2, 3, 4, 5, 6, 8
Z_WIDTH = D_IN - COL_W

SUBLANES = 8
LANES = 128
VMEM_LIMIT_BYTES = 56 * 1024 * 1024

INPROJ_TM = 512
HGRN_ROWS = 1024
HGRN_NC = 4
ATTN_ROWS = B_PAD
ATTN_QROWS = 2 * CHUNK
ATTN_WIN = B_PAD + ATTN_QROWS
POST_TM = 512
SUB = SUBLANES
N_SUB = CHUNK // SUB
SAFE_CHUNK_LOG2 = 115.0


def _params(*sem):
    return pltpu.CompilerParams(dimension_semantics=sem, vmem_limit_bytes=VMEM_LIMIT_BYTES)


def _resident(shape):
    nd = len(shape)
    return pl.BlockSpec(shape, lambda *_: (0,) * nd, pipeline_mode=pl.Buffered(1))


def _layer_weight(li, shape):
    nd = len(shape)
    return pl.BlockSpec((None,) + tuple(shape), lambda *_: (li,) + (0,) * nd,
                        pipeline_mode=pl.Buffered(1))


def _rmsnorm_bf16(x, g):
    ms = jnp.mean(x * x, axis=-1, keepdims=True)
    return (x * lax.rsqrt(ms + EPS) * g).astype(BF16)


def _sigmoid(x):
    return 1.0 / (1.0 + jnp.exp(-x))


def _pair_head_rms(t, g_pair):
    first = lax.broadcasted_iota(jnp.int32, t.shape, 1) < B_HD
    sq = t * t
    s0 = jnp.sum(jnp.where(first, sq, 0.0), axis=-1, keepdims=True)
    s1 = jnp.sum(jnp.where(first, 0.0, sq), axis=-1, keepdims=True)
    ms = jnp.where(first, s0, s1) * (1.0 / B_HD)
    return t * lax.rsqrt(ms + EPS) * g_pair


def _lower_bound(li, lbl):
    e = jnp.exp(lbl - jnp.max(lbl, axis=0, keepdims=True))
    soft = e / jnp.sum(e, axis=0, keepdims=True)
    lb = jnp.zeros((1, COL_W), F32)
    for r in range(1, li + 1):
        lb = lb + soft[r:r + 1, :]
    return lb


def _log2_forget(z, lb):
    t = jnp.exp(-jnp.abs(z))
    num = jnp.where(z >= 0.0, 1.0 + lb * t, lb + t)
    return jnp.where(num > 0.0, jnp.log2(num), z * LOG2E) - jnp.log2(1.0 + t)


def _inproj_kernel(li, x_ref, g_ref, w_ref, lbl_ref, qg_ref, kg_ref, z_ref, logf_ref):
    h = _rmsnorm_bf16(x_ref[...], g_ref[...])
    lb = _lower_bound(li, lbl_ref[...])
    n_slices = 4
    slice_rows = INPROJ_TM // n_slices

    def heads_normed(acc, g_pair, scale):
        tiles = [_pair_head_rms(acc[:, p * LANES:(p + 1) * LANES], g_pair) * scale
                 for p in range(COL_W // LANES)]
        return jnp.concatenate(tiles, axis=1)

    def finish(cb, acc):
        if cb == W_FA:
            logf_ref[...] = acc
            return
        if cb == W_GA:
            acc = acc * _sigmoid(acc)
        elif cb == W_QB:
            acc = heads_normed(acc, qg_ref[...], ATTN_Q_SCALE)
        elif cb == W_KB:
            acc = heads_normed(acc, kg_ref[...], 1.0)
        zc = Z_OF_W[cb]
        z_ref[:, zc * COL_W:(zc + 1) * COL_W] = acc.astype(BF16)

    def forget_slice(s):
        rows = slice(s * slice_rows, (s + 1) * slice_rows)
        logf_ref[rows, :] = _log2_forget(logf_ref[rows, :], lb)

    pending = None
    for cb in range(N_W_COLS):
        acc = jnp.dot(h, w_ref[:, cb * COL_W:(cb + 1) * COL_W], preferred_element_type=F32)
        if pending is not None:
            finish(*pending)
        pending = (cb, acc)
        s = cb - (W_FA + 2)
        if 0 <= s < n_slices:
            forget_slice(s)
    finish(*pending)


def _inproj(li, x, g, w, lb_logits, qg2, kg2):
    tm = INPROJ_TM
    return pl.pallas_call(
        functools.partial(_inproj_kernel, li),
        grid=(TOKENS // tm,),
        in_specs=[
            pl.BlockSpec((tm, D_MODEL), lambda i: (i, 0)),
            _resident((1, D_MODEL)),
            _layer_weight(li, (D_MODEL, D_IN)),
            _resident((DEPTH, COL_W)),
            _resident((1, LANES)),
            _resident((1, LANES)),
        ],
        out_specs=[
            pl.BlockSpec((tm, Z_WIDTH), lambda i: (i, 0)),
            pl.BlockSpec((tm, COL_W), lambda i: (i, 0)),
        ],
        out_shape=[
            jax.ShapeDtypeStruct((TOKENS, Z_WIDTH), BF16),
            jax.ShapeDtypeStruct((TOKENS, COL_W), F32),
        ],
        compiler_params=_params("arbitrary"),
        name="inproj",
    )(x, g, w, lb_logits, qg2, kg2)


def _split_bf16(v, terms):
    parts = []
    for _ in range(terms - 1):
        p = v.astype(BF16)
        parts.append(p)
        v = v - p.astype(F32)
    parts.append(v.astype(BF16))
    return parts


def _chunk_cumsum(lf, terms):
    nrow = lf.shape[0]
    row = lax.broadcasted_iota(jnp.int32, (nrow, nrow), 0)
    col = lax.broadcasted_iota(jnp.int32, (nrow, nrow), 1)
    shift = int(math.log2(CHUNK))
    same_chunk = lax.shift_right_logical(row, shift) == lax.shift_right_logical(col, shift)
    tril = jnp.where((row >= col) & same_chunk, 1.0, 0.0).astype(BF16)
    out = None
    for part in _split_bf16(lf, terms):
        d = jnp.dot(tril, part, preferred_element_type=F32)
        out = d if out is None else out + d
    return out


class _HgrnGroup:
    def __init__(self, refs, r0):
        self.refs, self.r0 = refs, r0
        self.units = [(h, c) for h in range(A_HEADS) for c in range(HGRN_NC)]

    @staticmethod
    def _hcols(h):
        return slice(h * A_DK, (h + 1) * A_DK)

    @staticmethod
    def _crows(c):
        return slice(c * CHUNK, (c + 1) * CHUNK)

    def decays(self):
        q_ref, lf_ref, i_ref, gs_ref = self.refs[:4]
        rows = pl.ds(self.r0, HGRN_NC * CHUNK)
        lf = lf_ref[rows, :]
        b = _chunk_cumsum(lf, 2)
        k = 1.0 - jnp.exp2(lf)
        qs = q_ref[rows, :].astype(F32) * (A_DK ** -0.5)
        self.q_dec = (qs * jnp.exp2(b)).astype(BF16)
        self.k_grow = (k * jnp.exp2(-b)).astype(BF16)
        self.b_last = [b[(c + 1) * CHUNK - 1:(c + 1) * CHUNK, :] for c in range(HGRN_NC)]
        self.k_dec = jnp.concatenate(
            [k[self._crows(c), :] * jnp.exp2(self.b_last[c] - b[self._crows(c), :])
             for c in range(HGRN_NC)], axis=0).astype(BF16)
        self.iv = i_ref[rows, :]
        self.gs = gs_ref[rows, :]

    def pair_products(self):
        hc, cr = self._hcols, self._crows
        self.attn = {(h, c): lax.dot_general(self.q_dec[cr(c), hc(h)], self.k_grow[cr(c), hc(h)],
                                             (((1,), (1,)), ((), ())), preferred_element_type=F32)
                     for h, c in self.units}
        self.upd = {(h, c): lax.dot_general(self.iv[cr(c), hc(h)], self.k_dec[cr(c), hc(h)],
                                            (((0,), (0,)), ((), ())), preferred_element_type=F32)
                    for h, c in self.units}

    def state(self):
        st_ref = self.refs[6]
        hc, cr = self._hcols, self._crows
        self.o_state = {}
        for h in range(A_HEADS):
            st = st_ref[h]
            for c in range(HGRN_NC):
                self.o_state[h, c] = lax.dot_general(
                    self.q_dec[cr(c), hc(h)], st.astype(BF16), (((1,), (1,)), ((), ())),
                    preferred_element_type=F32)
                st = st * jnp.exp2(self.b_last[c][:, hc(h)]) + self.upd[h, c]
            st_ref[h] = st

    def outputs(self):
        og_ref, out_ref = self.refs[4], self.refs[5]
        hc, cr = self._hcols, self._crows
        causal = (lax.broadcasted_iota(jnp.int32, (CHUNK, CHUNK), 0)
                  >= lax.broadcasted_iota(jnp.int32, (CHUNK, CHUNK), 1))
        for h, c in self.units:
            a = jnp.where(causal, self.attn[h, c], 0.0).astype(BF16)
            o = jnp.dot(a, self.iv[cr(c), hc(h)], preferred_element_type=F32) + self.o_state[h, c]
            ms = jnp.mean(o * o, axis=-1, keepdims=True)
            y = o * lax.rsqrt(ms + EPS) * og_ref[...] * self.gs[cr(c), hc(h)].astype(F32)
            out_ref[pl.ds(self.r0 + c * CHUNK, CHUNK), hc(h)] = y.astype(BF16)


def _hgrn_fast(refs):
    groups = [_HgrnGroup(refs, g * HGRN_NC * CHUNK) for g in range(HGRN_ROWS // (HGRN_NC * CHUNK))]
    groups[0].decays()
    groups[0].pair_products()
    for g, grp in enumerate(groups):
        if g + 1 < len(groups):
            groups[g + 1].decays()
        grp.state()
        grp.outputs()
        if g + 1 < len(groups):
            groups[g + 1].pair_products()


def _hgrn_head_chunk_exact(qs, lf, iv, st):
    b = _chunk_cumsum(lf, 3)
    k = 1.0 - jnp.exp2(lf)
    b_last = b[CHUNK - 1:CHUNK, :]

    o = lax.dot_general((qs * jnp.exp2(b)).astype(BF16), st.astype(BF16),
                        (((1,), (1,)), ((), ())), preferred_element_type=F32)

    zero_blk = jnp.zeros((SUB, A_DK), F32)
    c_rows = [None] + [b[SUB * i - 1:SUB * i, :] for i in range(1, N_SUB)]
    cref = jnp.concatenate(
        [zero_blk] + [jnp.broadcast_to(c_rows[i], (SUB, A_DK)) for i in range(1, N_SUB)], axis=0)
    q_all = qs * jnp.exp2(b - cref)
    q_parts, k_parts = [], []
    for i in range(1, N_SUB):
        q_parts.append(jnp.concatenate(
            [zero_blk] * i + [q_all[SUB * i:SUB * (i + 1), :]] + [zero_blk] * (N_SUB - 1 - i), axis=0))
        kd = k[:SUB * i, :] * jnp.exp2(c_rows[i] - b[:SUB * i, :])
        k_parts.append(jnp.concatenate([kd, jnp.zeros((CHUNK - SUB * i, A_DK), F32)], axis=0))
    q_cat = jnp.concatenate(q_parts, axis=1).astype(BF16)
    k_cat = jnp.concatenate(k_parts, axis=1).astype(BF16)
    attn = lax.dot_general(q_cat, k_cat, (((1,), (1,)), ((), ())),
                           preferred_element_type=F32)

    t_loc = lax.broadcasted_iota(jnp.int32, (SUB, CHUNK), 0)
    lane = lax.broadcasted_iota(jnp.int32, (SUB, CHUNK), 1)
    diag_blocks = []
    for i in range(N_SUB):
        qi = qs[SUB * i:SUB * (i + 1), :]
        bi = b[SUB * i:SUB * (i + 1), :]
        blk = jnp.zeros((SUB, CHUNK), F32)
        for jj in range(SUB):
            s = SUB * i + jj
            p = qi * k[s:s + 1, :] * jnp.exp2(bi - b[s:s + 1, :])
            colsum = jnp.sum(p, axis=-1, keepdims=True)
            blk = jnp.where((lane == s) & (t_loc >= jj), colsum, blk)
        diag_blocks.append(blk)
    attn = attn + jnp.concatenate(diag_blocks, axis=0)

    o = o + jnp.dot(attn.astype(BF16), iv, preferred_element_type=F32)

    k_dec = (k * jnp.exp2(b_last - b)).astype(BF16)
    st_new = st * jnp.exp2(b_last) + lax.dot_general(
        iv, k_dec, (((0,), (0,)), ((), ())), preferred_element_type=F32)
    return o, st_new


def _hgrn_kernel(q_ref, lf_ref, i_ref, gs_ref, og_ref, out_ref, st_ref):
    @pl.when(pl.program_id(1) == 0)
    def _():
        st_ref[...] = jnp.zeros_like(st_ref)

    n_chunks = HGRN_ROWS // CHUNK
    chunk_of_row = lax.shift_right_logical(
        lax.broadcasted_iota(jnp.int32, (n_chunks, HGRN_ROWS), 1), int(math.log2(CHUNK)))
    ind = jnp.where(chunk_of_row == lax.broadcasted_iota(jnp.int32, (n_chunks, HGRN_ROWS), 0),
                    1.0, 0.0).astype(BF16)
    chunk_sums = jnp.dot(ind, lf_ref[...].astype(BF16), preferred_element_type=F32)
    safe = jnp.min(chunk_sums) >= -SAFE_CHUNK_LOG2

    @pl.when(safe)
    def _():
        _hgrn_fast((q_ref, lf_ref, i_ref, gs_ref, og_ref, out_ref, st_ref))

    @pl.when(jnp.logical_not(safe))
    def _():
        def chunk_body(c, carry):
            r0 = pl.multiple_of(c * CHUNK, CHUNK)
            rows = pl.ds(r0, CHUNK)
            for h in range(A_HEADS):
                cols = slice(h * A_DK, (h + 1) * A_DK)
                qs = q_ref[rows, cols].astype(F32) * (A_DK ** -0.5)
                o, st_new = _hgrn_head_chunk_exact(qs, lf_ref[rows, cols], i_ref[rows, cols], st_ref[h])
                st_ref[h] = st_new
                ms = jnp.mean(o * o, axis=-1, keepdims=True)
                y = o * lax.rsqrt(ms + EPS) * og_ref[...] * gs_ref[rows, cols].astype(F32)
                out_ref[rows, cols] = y.astype(BF16)
            return carry

        lax.fori_loop(0, HGRN_ROWS // CHUNK, chunk_body, 0)


def _hgrn(z, logf, onorm_g):
    nb = SEQ // HGRN_ROWS

    def zcol(cb):
        return pl.BlockSpec((HGRN_ROWS, COL_W), lambda b, n: (b * nb + n, cb))

    return pl.pallas_call(
        _hgrn_kernel,
        grid=(BATCH, nb),
        in_specs=[
            zcol(Z_QA),
            pl.BlockSpec((HGRN_ROWS, COL_W), lambda b, n: (b * nb + n, 0)),
            zcol(Z_IA),
            zcol(Z_GA),
            _resident((1, A_DV)),
        ],
        out_specs=pl.BlockSpec((HGRN_ROWS, A_WIDTH), lambda b, n: (b * nb + n, 0)),
        out_shape=jax.ShapeDtypeStruct((TOKENS, A_WIDTH), BF16),
        scratch_shapes=[pltpu.VMEM((A_HEADS, A_DV, A_DK), F32)],
        compiler_params=_params("arbitrary", "arbitrary"),
        name="hgrn2",
    )(z, logf, z, z, onorm_g)


def _attn_kernel(q_ref, kp_ref, kc_ref, vp_ref, vc_ref, bias_ref, out_ref):
    n = pl.program_id(1)
    rows = ATTN_ROWS
    first_q = lax.broadcasted_iota(jnp.int32, (ATTN_QROWS, LANES), 1) < B_HD

    def run(first_block):
        units = [(u, p) for u in range(rows // ATTN_QROWS) for p in range(B_HEADS // 2)]

        def qrows(u):
            return slice(u * ATTN_QROWS, (u + 1) * ATTN_QROWS)

        def prev_rows(u):
            return slice(u * ATTN_QROWS, rows)

        def cur_rows(u):
            return slice(0, (u + 1) * ATTN_QROWS)

        def cols(p):
            return slice(p * LANES, (p + 1) * LANES)

        def key_dot(a, b_ref, r, p):
            return lax.dot_general(a, b_ref[r, cols(p)], (((1,), (1,)), ((), ())),
                                   preferred_element_type=F32)

        def score(u, p):
            qp = q_ref[qrows(u), cols(p)]
            zero = jnp.zeros_like(qp)
            qq = jnp.concatenate([jnp.where(first_q, qp, zero), jnp.where(first_q, zero, qp)], axis=0)
            s_cur = key_dot(qq, kc_ref, cur_rows(u), p)
            if first_block:
                return s_cur
            return jnp.concatenate([key_dot(qq, kp_ref, prev_rows(u), p), s_cur], axis=1)

        def softmax(u, p, s):
            n_prev = 0 if first_block else rows - u * ATTN_QROWS
            s = s + bias_ref[p, :, ATTN_WIN - (n_prev + (u + 1) * ATTN_QROWS):]
            m = jnp.max(s, axis=-1, keepdims=True)
            e = jnp.exp2(s - m)
            return e.astype(BF16), jnp.sum(e, axis=-1, keepdims=True)

        def emit(u, p, prob, denom):
            n_prev = 0 if first_block else rows - u * ATTN_QROWS
            o = jnp.dot(prob[:, n_prev:], vc_ref[cur_rows(u), cols(p)], preferred_element_type=F32)
            if not first_block:
                o = o + jnp.dot(prob[:, :n_prev], vp_ref[prev_rows(u), cols(p)],
                                preferred_element_type=F32)
            o = o / denom
            out_ref[qrows(u), cols(p)] = jnp.where(
                first_q, o[:ATTN_QROWS], o[ATTN_QROWS:]).astype(BF16)

        scores, soft = {}, {}
        for step in range(len(units) + 3):
            if step < len(units):
                scores[step] = score(*units[step])
            if 0 <= step - 2 < len(units):
                soft[step - 2] = softmax(*units[step - 2], scores.pop(step - 2))
            if 0 <= step - 3 < len(units):
                emit(*units[step - 3], *soft.pop(step - 3))

    @pl.when(n == 0)
    def _():
        run(True)

    @pl.when(n > 0)
    def _():
        run(False)


def _band_bias(rel_bias):
    rb = rel_bias.astype(F32)
    n_far = (CHUNK - 1) + (B_PAD - MAX_REL) + 1
    far = jnp.broadcast_to(rb[:, 2 * MAX_REL:], (B_HEADS, n_far))
    near = rb[:, MAX_REL - (CHUNK - 1):2 * MAX_REL][:, ::-1]
    tb = jnp.concatenate([far, near], axis=1)
    band = jnp.stack([tb[:, CHUNK - 1 - c:CHUNK - 1 - c + B_BAND] for c in range(CHUNK)], axis=1)
    lo = jnp.pad(band, ((0, 0), (0, 0), (0, CHUNK)), constant_values=NEG_BIG)
    hi = jnp.pad(band, ((0, 0), (0, 0), (CHUNK, 0)), constant_values=NEG_BIG)
    unit = jnp.concatenate([lo, hi], axis=1)
    return (unit * LOG2E).reshape(B_HEADS // 2, 2 * ATTN_QROWS, ATTN_WIN)


def _attn(z, bias):
    rows = ATTN_ROWS
    nb = SEQ // rows

    def cur(cb):
        return pl.BlockSpec((rows, COL_W), lambda b, n: (b * nb + n, cb))

    def prev(cb):
        return pl.BlockSpec((rows, COL_W), lambda b, n: (b * nb + jnp.maximum(n - 1, 0), cb))

    return pl.pallas_call(
        _attn_kernel,
        grid=(BATCH, nb),
        in_specs=[
            cur(Z_QB), prev(Z_KB), cur(Z_KB), prev(Z_VB), cur(Z_VB),
            _resident((B_HEADS // 2, 2 * ATTN_QROWS, ATTN_WIN)),
        ],
        out_specs=pl.BlockSpec((rows, B_WIDTH), lambda b, n: (b * nb + n, 0)),
        out_shape=jax.ShapeDtypeStruct((TOKENS, B_WIDTH), BF16),
        compiler_params=_params("arbitrary", "arbitrary"),
        name="band_attn",
    )(z, z, z, z, z, bias)


def _post_kernel(x_ref, oa_ref, ob_ref, ga0_ref, ga1_ref, gb0_ref, gb1_ref, p_ref,
                 wa_ref, wb_ref, wo_ref, gf_ref, wg_ref, wu_ref, wd_ref, gp_ref, wpg_ref, wpp_ref,
                 out_ref):
    ga = jnp.concatenate([ga0_ref[...], ga1_ref[...]], axis=1).astype(F32)
    gb = jnp.concatenate([gb0_ref[...], gb1_ref[...]], axis=1).astype(F32)
    pa = jnp.dot(oa_ref[...], wa_ref[...], preferred_element_type=F32)
    pb = jnp.dot(ob_ref[...], wb_ref[...], preferred_element_type=F32)
    merged = (_sigmoid(ga) * pa + _sigmoid(gb) * pb).astype(BF16)
    x1 = x_ref[...] + jnp.dot(merged, wo_ref[...], preferred_element_type=F32)
    h = _rmsnorm_bf16(x1, gf_ref[...])
    gate = jnp.dot(h, wg_ref[...], preferred_element_type=F32)
    up = jnp.dot(h, wu_ref[...], preferred_element_type=F32)
    act = (gate * _sigmoid(gate) * up).astype(BF16)
    x2 = x1 + jnp.dot(act, wd_ref[...], preferred_element_type=F32)
    emb = jnp.dot(p_ref[...].astype(BF16), wpp_ref[...], preferred_element_type=F32)
    h2 = _rmsnorm_bf16(x2, gp_ref[...])
    pgate = _sigmoid(jnp.dot(h2, wpg_ref[...], preferred_element_type=F32))
    out_ref[...] = x2 + pgate * emb


def _post(li, x, oa, ob, z, p, wa, wb, wo, gf, wg, wu, wd, gp, wpg, wpp):
    tm = POST_TM

    def zcol(cb):
        return pl.BlockSpec((tm, COL_W), lambda i: (i, cb))

    return pl.pallas_call(
        _post_kernel,
        grid=(TOKENS // tm,),
        in_specs=[
            pl.BlockSpec((tm, D_MODEL), lambda i: (i, 0)),
            pl.BlockSpec((tm, A_WIDTH), lambda i: (i, 0)),
            pl.BlockSpec((tm, B_WIDTH), lambda i: (i, 0)),
            zcol(Z_GATE_A), zcol(Z_GATE_A + 1), zcol(Z_GATE_B), zcol(Z_GATE_B + 1),
            pl.BlockSpec((None, tm, PLE_DIM), lambda i: (li, i, 0)),
            _layer_weight(li, (A_WIDTH, D_MODEL)),
            _layer_weight(li, (B_WIDTH, D_MODEL)),
            _layer_weight(li, (D_MODEL, D_MODEL)),
            _resident((1, D_MODEL)),
            _layer_weight(li, (D_MODEL, D_FF)),
            _layer_weight(li, (D_MODEL, D_FF)),
            _layer_weight(li, (D_FF, D_MODEL)),
            _resident((1, D_MODEL)),
            _layer_weight(li, (D_MODEL, D_MODEL)),
            _layer_weight(li, (PLE_DIM, D_MODEL)),
        ],
        out_specs=pl.BlockSpec((tm, D_MODEL), lambda i: (i, 0)),
        out_shape=jax.ShapeDtypeStruct((TOKENS, D_MODEL), F32),
        compiler_params=_params("arbitrary"),
        name="post_mix",
    )(x, oa, ob, z, z, z, z, p, wa, wb, wo, gf, wg, wu, wd, gp, wpg, wpp)


def kernel(x, p, norm_mix_g, w_in, hgrn_lb_logits, hgrn_onorm_g, attn_qnorm_g, attn_knorm_g,
           attn_rel_bias, w_branch_a, w_branch_b, w_out, norm_ffn_g, w_ffn_gate, w_ffn_up,
           w_ffn_down, norm_ple_g, w_ple_gate, w_ple_proj):
    xt = x.reshape(TOKENS, D_MODEL)
    pt = p.reshape(DEPTH, TOKENS, PLE_DIM)
    w_in, w_a, w_b, w_o, w_g, w_u, w_d, w_pg, w_pp = (
        w.astype(BF16) for w in (w_in, w_branch_a, w_branch_b, w_out, w_ffn_gate, w_ffn_up,
                                 w_ffn_down, w_ple_gate, w_ple_proj))
    for li in range(DEPTH):
        qg2 = jnp.concatenate([attn_qnorm_g[li], attn_qnorm_g[li]]).reshape(1, LANES)
        kg2 = jnp.concatenate([attn_knorm_g[li], attn_knorm_g[li]]).reshape(1, LANES)
        z, logf = _inproj(li, xt, norm_mix_g[li].reshape(1, D_MODEL), w_in, hgrn_lb_logits, qg2, kg2)
        oa = _hgrn(z, logf, hgrn_onorm_g[li].reshape(1, A_DV))
        ob = _attn(z, _band_bias(attn_rel_bias[li]))
        xt = _post(li, xt, oa, ob, z, pt, w_a, w_b, w_o, norm_ffn_g[li].reshape(1, D_MODEL),
                   w_g, w_u, w_d, norm_ple_g[li].reshape(1, D_MODEL), w_pg, w_pp)
    return xt.reshape(BATCH, SEQ, D_MODEL)
```

```python
import functools
import math

import jax
import jax.numpy as jnp
from jax import lax
from jax.experimental import pallas as pl
from jax.experimental.pallas import tpu as pltpu

F32 = jnp.float32
BF16 = jnp.bfloat16

D_MODEL = 1024
BATCH = 8
SEQ = 4096
DEPTH = 4
TOKENS = BATCH * SEQ
CHUNK = 64
PLE_DIM = 256
EPS = 1e-6
A_HEADS = 4
A_DK = 128
A_DV = 128
A_WIDTH = A_HEADS * A_DV
B_HEADS = 8
B_HD = 64
B_WIDTH = B_HEADS * B_HD
B_LEFT_CHUNKS = 8
B_PAD = B_LEFT_CHUNKS * CHUNK
B_BAND = B_PAD + CHUNK
MAX_REL = 128
D_FF = 2816
D_IN = 5632
NEG_BIG = -1e30
LOG2E = math.log2(math.e)
ATTN_Q_SCALE = B_HD ** -0.5 * LOG2E

COL_W = 512
W_QA, W_FA, W_IA, W_GA, W_QB, W_KB, W_VB, W_GATE_A, W_GATE_B = 0, 1, 2, 3, 4, 5, 6, 7, 9
N_W_COLS = D_IN // COL_W
Z_OF_W = {0: 0, 2: 1, 3: 2, 4: 3, 5: 4, 6: 5, 7: 6, 8: 7, 9: 8, 10: 9}
Z_QA, Z_IA, Z_GA, Z_QB, Z_KB, Z_VB, Z_GATE_A, Z_GATE_B = 0, 1, 2, 3, 4, 5, 6, 8
Z_WIDTH = D_IN - COL_W

SUBLANES = 8
LANES = 128
VMEM_LIMIT_BYTES = 56 * 1024 * 1024

INPROJ_TM = 512
HGRN_ROWS = 1024
HGRN_NC = 2
ATTN_ROWS = B_PAD
ATTN_QROWS = 2 * CHUNK
ATTN_WIN = B_PAD + ATTN_QROWS
POST_TM = 512
POST_SLICES = 2
SUB = SUBLANES
N_SUB = CHUNK // SUB
SAFE_CHUNK_LOG2 = 115.0


def _params(*sem):
    return pltpu.CompilerParams(dimension_semantics=sem, vmem_limit_bytes=VMEM_LIMIT_BYTES)


def _resident(shape):
    nd = len(shape)
    return pl.BlockSpec(shape, lambda *_: (0,) * nd, pipeline_mode=pl.Buffered(1))


def _layer_weight(li, shape):
    nd = len(shape)
    return pl.BlockSpec((None,) + tuple(shape), lambda *_: (li,) + (0,) * nd,
                        pipeline_mode=pl.Buffered(1))


def _rmsnorm_bf16(x, g):
    ms = jnp.mean(x * x, axis=-1, keepdims=True)
    return (x * lax.rsqrt(ms + EPS) * g).astype(BF16)


def _sigmoid(x):
    return 1.0 / (1.0 + jnp.exp(-x))


def _pair_head_rms(t, g_pair):
    first = lax.broadcasted_iota(jnp.int32, t.shape, 1) < B_HD
    sq = t * t
    s0 = jnp.sum(jnp.where(first, sq, 0.0), axis=-1, keepdims=True)
    s1 = jnp.sum(jnp.where(first, 0.0, sq), axis=-1, keepdims=True)
    ms = jnp.where(first, s0, s1) * (1.0 / B_HD)
    return t * lax.rsqrt(ms + EPS) * g_pair


def _lower_bound(li, lbl):
    e = jnp.exp(lbl - jnp.max(lbl, axis=0, keepdims=True))
    soft = e / jnp.sum(e, axis=0, keepdims=True)
    lb = jnp.zeros((1, COL_W), F32)
    for r in range(1, li + 1):
        lb = lb + soft[r:r + 1, :]
    return lb


def _log2_forget(z, lb):
    t = jnp.exp(-jnp.abs(z))
    num = jnp.where(z >= 0.0, 1.0 + lb * t, lb + t)
    return jnp.where(num > 0.0, jnp.log2(num), z * LOG2E) - jnp.log2(1.0 + t)


def _inproj_kernel(li, x_ref, g_ref, w_ref, lbl_ref, qg_ref, kg_ref, z_ref, logf_ref):
    h = _rmsnorm_bf16(x_ref[...], g_ref[...])
    lb = _lower_bound(li, lbl_ref[...])
    n_slices = 8
    slice_rows = INPROJ_TM // n_slices

    def heads_normed(acc, g_pair, scale):
        tiles = [_pair_head_rms(acc[:, p * LANES:(p + 1) * LANES], g_pair) * scale
                 for p in range(COL_W // LANES)]
        return jnp.concatenate(tiles, axis=1)

    def finish(cb, acc):
        if cb == W_FA:
            logf_ref[...] = acc
            return
        if cb == W_GA:
            acc = acc * _sigmoid(acc)
        elif cb == W_QB:
            acc = heads_normed(acc, qg_ref[...], ATTN_Q_SCALE)
        elif cb == W_KB:
            acc = heads_normed(acc, kg_ref[...], 1.0)
        zc = Z_OF_W[cb]
        z_ref[:, zc * COL_W:(zc + 1) * COL_W] = acc.astype(BF16)

    def forget_slice(s, after):
        rows = slice(s * slice_rows, (s + 1) * slice_rows)
        bits = pltpu.bitcast(after[0:SUBLANES, :], jnp.uint32)
        zero = lax.shift_right_logical(lax.shift_right_logical(bits, jnp.uint32(16)), jnp.uint32(16))
        lb_tied = lb + zero[0:1, :].astype(F32)
        logf_ref[rows, :] = _log2_forget(logf_ref[rows, :], lb_tied)

    pending = None
    for cb in range(N_W_COLS):
        acc = jnp.dot(h, w_ref[:, cb * COL_W:(cb + 1) * COL_W], preferred_element_type=F32)
        if pending is not None:
            finish(*pending)
        pending = (cb, acc)
        s = cb - (N_W_COLS - n_slices)
        if 0 <= s < n_slices:
            forget_slice(s, acc)
    finish(*pending)


def _inproj(li, x, g, w, lb_logits, qg2, kg2):
    tm = INPROJ_TM
    return pl.pallas_call(
        functools.partial(_inproj_kernel, li),
        grid=(TOKENS // tm,),
        in_specs=[
            pl.BlockSpec((tm, D_MODEL), lambda i: (i, 0)),
            _resident((1, D_MODEL)),
            _layer_weight(li, (D_MODEL, D_IN)),
            _resident((DEPTH, COL_W)),
            _resident((1, LANES)),
            _resident((1, LANES)),
        ],
        out_specs=[
            pl.BlockSpec((tm, Z_WIDTH), lambda i: (i, 0)),
            pl.BlockSpec((tm, COL_W), lambda i: (i, 0)),
        ],
        out_shape=[
            jax.ShapeDtypeStruct((TOKENS, Z_WIDTH), BF16),
            jax.ShapeDtypeStruct((TOKENS, COL_W), F32),
        ],
        compiler_params=_params("arbitrary"),
        name="inproj",
    )(x, g, w, lb_logits, qg2, kg2)


def _split_bf16(v, terms):
    parts = []
    for _ in range(terms - 1):
        p = v.astype(BF16)
        parts.append(p)
        v = v - p.astype(F32)
    parts.append(v.astype(BF16))
    return parts


def _chunk_cumsum(lf, terms):
    nrow = lf.shape[0]
    row = lax.broadcasted_iota(jnp.int32, (nrow, nrow), 0)
    col = lax.broadcasted_iota(jnp.int32, (nrow, nrow), 1)
    shift = int(math.log2(CHUNK))
    same_chunk = lax.shift_right_logical(row, shift) == lax.shift_right_logical(col, shift)
    tril = jnp.where((row >= col) & same_chunk, 1.0, 0.0).astype(BF16)
    out = None
    for part in _split_bf16(lf, terms):
        d = jnp.dot(tril, part, preferred_element_type=F32)
        out = d if out is None else out + d
    return out


class _HgrnGroup:
    def __init__(self, refs, r0):
        self.refs, self.r0 = refs, r0
        self.units = [(h, c) for h in range(A_HEADS) for c in range(HGRN_NC)]

    @staticmethod
    def _hcols(h):
        return slice(h * A_DK, (h + 1) * A_DK)

    @staticmethod
    def _crows(c):
        return slice(c * CHUNK, (c + 1) * CHUNK)

    def decays(self):
        q_ref, lf_ref, i_ref, gs_ref = self.refs[:4]
        rows = pl.ds(self.r0, HGRN_NC * CHUNK)
        lf = lf_ref[rows, :]
        b = _chunk_cumsum(lf, 2)
        k = 1.0 - jnp.exp2(lf)
        qs = q_ref[rows, :].astype(F32) * (A_DK ** -0.5)
        self.q_dec = (qs * jnp.exp2(b)).astype(BF16)
        self.k_grow = (k * jnp.exp2(-b)).astype(BF16)
        self.b_last = [b[(c + 1) * CHUNK - 1:(c + 1) * CHUNK, :] for c in range(HGRN_NC)]
        self.k_dec = jnp.concatenate(
            [k[self._crows(c), :] * jnp.exp2(self.b_last[c] - b[self._crows(c), :])
             for c in range(HGRN_NC)], axis=0).astype(BF16)
        self.iv = i_ref[rows, :]
        self.gs = gs_ref[rows, :]

    def pair_products(self):
        hc, cr = self._hcols, self._crows
        self.attn = {(h, c): lax.dot_general(self.q_dec[cr(c), hc(h)], self.k_grow[cr(c), hc(h)],
                                             (((1,), (1,)), ((), ())), preferred_element_type=F32)
                     for h, c in self.units}
        self.upd = {(h, c): lax.dot_general(self.iv[cr(c), hc(h)], self.k_dec[cr(c), hc(h)],
                                            (((0,), (0,)), ((), ())), preferred_element_type=F32)
                    for h, c in self.units}

    def state(self):
        st_ref = self.refs[6]
        hc, cr = self._hcols, self._crows
        self.o_state = {}
        for h in range(A_HEADS):
            st = st_ref[h]
            for c in range(HGRN_NC):
                self.o_state[h, c] = lax.dot_general(
                    self.q_dec[cr(c), hc(h)], st.astype(BF16), (((1,), (1,)), ((), ())),
                    preferred_element_type=F32)
                st = st * jnp.exp2(self.b_last[c][:, hc(h)]) + self.upd[h, c]
            st_ref[h] = st

    def outputs(self):
        og_ref, out_ref = self.refs[4], self.refs[5]
        hc, cr = self._hcols, self._crows
        causal = (lax.broadcasted_iota(jnp.int32, (CHUNK, CHUNK), 0)
                  >= lax.broadcasted_iota(jnp.int32, (CHUNK, CHUNK), 1))
        for h, c in self.units:
            a = jnp.where(causal, self.attn[h, c], 0.0).astype(BF16)
            o = jnp.dot(a, self.iv[cr(c), hc(h)], preferred_element_type=F32) + self.o_state[h, c]
            ms = jnp.mean(o * o, axis=-1, keepdims=True)
            y = o * lax.rsqrt(ms + EPS) * og_ref[...] * self.gs[cr(c), hc(h)].astype(F32)
            out_ref[pl.ds(self.r0 + c * CHUNK, CHUNK), hc(h)] = y.astype(BF16)


def _hgrn_fast(refs):
    groups = [_HgrnGroup(refs, g * HGRN_NC * CHUNK) for g in range(HGRN_ROWS // (HGRN_NC * CHUNK))]
    groups[0].decays()
    groups[0].pair_products()
    for g, grp in enumerate(groups):
        if g + 1 < len(groups):
            groups[g + 1].decays()
        grp.state()
        grp.outputs()
        if g + 1 < len(groups):
            groups[g + 1].pair_products()


def _hgrn_head_chunk_exact(qs, lf, iv, st):
    b = _chunk_cumsum(lf, 3)
    k = 1.0 - jnp.exp2(lf)
    b_last = b[CHUNK - 1:CHUNK, :]

    o = lax.dot_general((qs * jnp.exp2(b)).astype(BF16), st.astype(BF16),
                        (((1,), (1,)), ((), ())), preferred_element_type=F32)

    zero_blk = jnp.zeros((SUB, A_DK), F32)
    c_rows = [None] + [b[SUB * i - 1:SUB * i, :] for i in range(1, N_SUB)]
    cref = jnp.concatenate(
        [zero_blk] + [jnp.broadcast_to(c_rows[i], (SUB, A_DK)) for i in range(1, N_SUB)], axis=0)
    q_all = qs * jnp.exp2(b - cref)
    q_parts, k_parts = [], []
    for i in range(1, N_SUB):
        q_parts.append(jnp.concatenate(
            [zero_blk] * i + [q_all[SUB * i:SUB * (i + 1), :]] + [zero_blk] * (N_SUB - 1 - i), axis=0))
        kd = k[:SUB * i, :] * jnp.exp2(c_rows[i] - b[:SUB * i, :])
        k_parts.append(jnp.concatenate([kd, jnp.zeros((CHUNK - SUB * i, A_DK), F32)], axis=0))
    q_cat = jnp.concatenate(q_parts, axis=1).astype(BF16)
    k_cat = jnp.concatenate(k_parts, axis=1).astype(BF16)
    attn = lax.dot_general(q_cat, k_cat, (((1,), (1,)), ((), ())),
                           preferred_element_type=F32)

    t_loc = lax.broadcasted_iota(jnp.int32, (SUB, CHUNK), 0)
    lane = lax.broadcasted_iota(jnp.int32, (SUB, CHUNK), 1)
    diag_blocks = []
    for i in range(N_SUB):
        qi = qs[SUB * i:SUB * (i + 1), :]
        bi = b[SUB * i:SUB * (i + 1), :]
        blk = jnp.zeros((SUB, CHUNK), F32)
        for jj in range(SUB):
            s = SUB * i + jj
            p = qi * k[s:s + 1, :] * jnp.exp2(bi - b[s:s + 1, :])
            colsum = jnp.sum(p, axis=-1, keepdims=True)
            blk = jnp.where((lane == s) & (t_loc >= jj), colsum, blk)
        diag_blocks.append(blk)
    attn = attn + jnp.concatenate(diag_blocks, axis=0)

    o = o + jnp.dot(attn.astype(BF16), iv, preferred_element_type=F32)

    k_dec = (k * jnp.exp2(b_last - b)).astype(BF16)
    st_new = st * jnp.exp2(b_last) + lax.dot_general(
        iv, k_dec, (((0,), (0,)), ((), ())), preferred_element_type=F32)
    return o, st_new


def _hgrn_kernel(q_ref, lf_ref, i_ref, gs_ref, og_ref, out_ref, st_ref):
    @pl.when(pl.program_id(1) == 0)
    def _():
        st_ref[...] = jnp.zeros_like(st_ref)

    n_chunks = HGRN_ROWS // CHUNK
    chunk_of_row = lax.shift_right_logical(
        lax.broadcasted_iota(jnp.int32, (n_chunks, HGRN_ROWS), 1), int(math.log2(CHUNK)))
    ind = jnp.where(chunk_of_row == lax.broadcasted_iota(jnp.int32, (n_chunks, HGRN_ROWS), 0),
                    1.0, 0.0).astype(BF16)
    chunk_sums = jnp.dot(ind, lf_ref[...].astype(BF16), preferred_element_type=F32)
    safe = jnp.min(chunk_sums) >= -SAFE_CHUNK_LOG2

    @pl.when(safe)
    def _():
        _hgrn_fast((q_ref, lf_ref, i_ref, gs_ref, og_ref, out_ref, st_ref))

    @pl.when(jnp.logical_not(safe))
    def _():
        def chunk_body(c, carry):
            r0 = pl.multiple_of(c * CHUNK, CHUNK)
            rows = pl.ds(r0, CHUNK)
            for h in range(A_HEADS):
                cols = slice(h * A_DK, (h + 1) * A_DK)
                qs = q_ref[rows, cols].astype(F32) * (A_DK ** -0.5)
                o, st_new = _hgrn_head_chunk_exact(qs, lf_ref[rows, cols], i_ref[rows, cols], st_ref[h])
                st_ref[h] = st_new
                ms = jnp.mean(o * o, axis=-1, keepdims=True)
                y = o * lax.rsqrt(ms + EPS) * og_ref[...] * gs_ref[rows, cols].astype(F32)
                out_ref[rows, cols] = y.astype(BF16)
            return carry

        lax.fori_loop(0, HGRN_ROWS // CHUNK, chunk_body, 0)


def _hgrn(z, logf, onorm_g):
    nb = SEQ // HGRN_ROWS

    def zcol(cb):
        return pl.BlockSpec((HGRN_ROWS, COL_W), lambda b, n: (b * nb + n, cb))

    return pl.pallas_call(
        _hgrn_kernel,
        grid=(BATCH, nb),
        in_specs=[
            zcol(Z_QA),
            pl.BlockSpec((HGRN_ROWS, COL_W), lambda b, n: (b * nb + n, 0)),
            zcol(Z_IA),
            zcol(Z_GA),
            _resident((1, A_DV)),
        ],
        out_specs=pl.BlockSpec((HGRN_ROWS, A_WIDTH), lambda b, n: (b * nb + n, 0)),
        out_shape=jax.ShapeDtypeStruct((TOKENS, A_WIDTH), BF16),
        scratch_shapes=[pltpu.VMEM((A_HEADS, A_DV, A_DK), F32)],
        compiler_params=_params("arbitrary", "arbitrary"),
        name="hgrn2",
    )(z, logf, z, z, onorm_g)


def _attn_kernel(q_ref, kp_ref, kc_ref, vp_ref, vc_ref, bias_ref, out_ref):
    n = pl.program_id(1)
    rows = ATTN_ROWS
    first_q = lax.broadcasted_iota(jnp.int32, (ATTN_QROWS, LANES), 1) < B_HD

    def run(first_block):
        units = [(u, p) for u in range(rows // ATTN_QROWS) for p in range(B_HEADS // 2)]

        def qrows(u):
            return slice(u * ATTN_QROWS, (u + 1) * ATTN_QROWS)

        def prev_rows(u):
            return slice(u * ATTN_QROWS, rows)

        def cur_rows(u):
            return slice(0, (u + 1) * ATTN_QROWS)

        def cols(p):
            return slice(p * LANES, (p + 1) * LANES)

        def key_dot(a, b_ref, r, p):
            return lax.dot_general(a, b_ref[r, cols(p)], (((1,), (1,)), ((), ())),
                                   preferred_element_type=F32)

        def score(u, p):
            qp = q_ref[qrows(u), cols(p)]
            zero = jnp.zeros_like(qp)
            qq = jnp.concatenate([jnp.where(first_q, qp, zero), jnp.where(first_q, zero, qp)], axis=0)
            s_cur = key_dot(qq, kc_ref, cur_rows(u), p)
            if first_block:
                return s_cur
            return jnp.concatenate([key_dot(qq, kp_ref, prev_rows(u), p), s_cur], axis=1)

        def softmax(u, p, s):
            n_prev = 0 if first_block else rows - u * ATTN_QROWS
            s = s + bias_ref[p, :, ATTN_WIN - (n_prev + (u + 1) * ATTN_QROWS):]
            m = jnp.max(s, axis=-1, keepdims=True)
            e = jnp.exp2(s - m)
            return e.astype(BF16), jnp.sum(e, axis=-1, keepdims=True)

        def emit(u, p, prob, denom):
            n_prev = 0 if first_block else rows - u * ATTN_QROWS
            o = jnp.dot(prob[:, n_prev:], vc_ref[cur_rows(u), cols(p)], preferred_element_type=F32)
            if not first_block:
                o = o + jnp.dot(prob[:, :n_prev], vp_ref[prev_rows(u), cols(p)],
                                preferred_element_type=F32)
            o = o / denom
            out_ref[qrows(u), cols(p)] = jnp.where(
                first_q, o[:ATTN_QROWS], o[ATTN_QROWS:]).astype(BF16)

        scores, soft = {}, {}
        for step in range(len(units) + 2):
            if step < len(units):
                scores[step] = score(*units[step])
            if 0 <= step - 1 < len(units):
                soft[step - 1] = softmax(*units[step - 1], scores.pop(step - 1))
            if 0 <= step - 2 < len(units):
                emit(*units[step - 2], *soft.pop(step - 2))

    @pl.when(n == 0)
    def _():
        run(True)

    @pl.when(n > 0)
    def _():
        run(False)


def _band_bias(rel_bias):
    rb = rel_bias.astype(F32)
    n_far = (CHUNK - 1) + (B_PAD - MAX_REL) + 1
    far = jnp.broadcast_to(rb[:, 2 * MAX_REL:], (B_HEADS, n_far))
    near = rb[:, MAX_REL - (CHUNK - 1):2 * MAX_REL][:, ::-1]
    tb = jnp.concatenate([far, near], axis=1)
    band = jnp.stack([tb[:, CHUNK - 1 - c:CHUNK - 1 - c + B_BAND] for c in range(CHUNK)], axis=1)
    lo = jnp.pad(band, ((0, 0), (0, 0), (0, CHUNK)), constant_values=NEG_BIG)
    hi = jnp.pad(band, ((0, 0), (0, 0), (CHUNK, 0)), constant_values=NEG_BIG)
    unit = jnp.concatenate([lo, hi], axis=1)
    return (unit * LOG2E).reshape(B_HEADS // 2, 2 * ATTN_QROWS, ATTN_WIN)


def _attn(z, bias):
    rows = ATTN_ROWS
    nb = SEQ // rows

    def cur(cb):
        return pl.BlockSpec((rows, COL_W), lambda b, n: (b * nb + n, cb))

    def prev(cb):
        return pl.BlockSpec((rows, COL_W), lambda b, n: (b * nb + jnp.maximum(n - 1, 0), cb))

    return pl.pallas_call(
        _attn_kernel,
        grid=(BATCH, nb),
        in_specs=[
            cur(Z_QB), prev(Z_KB), cur(Z_KB), prev(Z_VB), cur(Z_VB),
            _resident((B_HEADS // 2, 2 * ATTN_QROWS, ATTN_WIN)),
        ],
        out_specs=pl.BlockSpec((rows, B_WIDTH), lambda b, n: (b * nb + n, 0)),
        out_shape=jax.ShapeDtypeStruct((TOKENS, B_WIDTH), BF16),
        compiler_params=_params("arbitrary", "arbitrary"),
        name="band_attn",
    )(z, z, z, z, z, bias)


def _post_kernel(x_ref, oa_ref, ob_ref, ga0_ref, ga1_ref, gb0_ref, gb1_ref, p_ref,
                 wa_ref, wb_ref, wo_ref, gf_ref, wg_ref, wu_ref, wd_ref, gp_ref, wpg_ref, wpp_ref,
                 out_ref):
    n_slices = POST_SLICES
    rows = [slice(s * (POST_TM // n_slices), (s + 1) * (POST_TM // n_slices)) for s in range(n_slices)]

    def merge(r):
        ga = jnp.concatenate([ga0_ref[r, :], ga1_ref[r, :]], axis=1).astype(F32)
        gb = jnp.concatenate([gb0_ref[r, :], gb1_ref[r, :]], axis=1).astype(F32)
        pa = jnp.dot(oa_ref[r, :], wa_ref[...], preferred_element_type=F32)
        pb = jnp.dot(ob_ref[r, :], wb_ref[...], preferred_element_type=F32)
        merged = (_sigmoid(ga) * pa + _sigmoid(gb) * pb).astype(BF16)
        return x_ref[r, :] + jnp.dot(merged, wo_ref[...], preferred_element_type=F32)

    def ffn(r, x1):
        h = _rmsnorm_bf16(x1, gf_ref[...])
        gate = jnp.dot(h, wg_ref[...], preferred_element_type=F32)
        up = jnp.dot(h, wu_ref[...], preferred_element_type=F32)
        act = (gate * _sigmoid(gate) * up).astype(BF16)
        return x1 + jnp.dot(act, wd_ref[...], preferred_element_type=F32)

    def ple(r, x2):
        emb = jnp.dot(p_ref[r, :].astype(BF16), wpp_ref[...], preferred_element_type=F32)
        h2 = _rmsnorm_bf16(x2, gp_ref[...])
        pgate = _sigmoid(jnp.dot(h2, wpg_ref[...], preferred_element_type=F32))
        out_ref[r, :] = x2 + pgate * emb

    x1 = [merge(r) for r in rows]
    x2 = [ffn(r, v) for r, v in zip(rows, x1)]
    for r, v in zip(rows, x2):
        ple(r, v)


def _post(li, x, oa, ob, z, p, wa, wb, wo, gf, wg, wu, wd, gp, wpg, wpp):
    tm = POST_TM

    def zcol(cb):
        return pl.BlockSpec((tm, COL_W), lambda i: (i, cb))

    return pl.pallas_call(
        _post_kernel,
        grid=(TOKENS // tm,),
        in_specs=[
            pl.BlockSpec((tm, D_MODEL), lambda i: (i, 0)),
            pl.BlockSpec((tm, A_WIDTH), lambda i: (i, 0)),
            pl.BlockSpec((tm, B_WIDTH), lambda i: (i, 0)),
            zcol(Z_GATE_A), zcol(Z_GATE_A + 1), zcol(Z_GATE_B), zcol(Z_GATE_B + 1),
            pl.BlockSpec((None, tm, PLE_DIM), lambda i: (li, i, 0)),
            _layer_weight(li, (A_WIDTH, D_MODEL)),
            _layer_weight(li, (B_WIDTH, D_MODEL)),
            _layer_weight(li, (D_MODEL, D_MODEL)),
            _resident((1, D_MODEL)),
            _layer_weight(li, (D_MODEL, D_FF)),
            _layer_weight(li, (D_MODEL, D_FF)),
            _layer_weight(li, (D_FF, D_MODEL)),
            _resident((1, D_MODEL)),
            _layer_weight(li, (D_MODEL, D_MODEL)),
            _layer_weight(li, (PLE_DIM, D_MODEL)),
        ],
        out_specs=pl.BlockSpec((tm, D_MODEL), lambda i: (i, 0)),
        out_shape=jax.ShapeDtypeStruct((TOKENS, D_MODEL), F32),
        compiler_params=_params("arbitrary"),
        name="post_mix",
    )(x, oa, ob, z, z, z, z, p, wa, wb, wo, gf, wg, wu, wd, gp, wpg, wpp)


def kernel(x, p, norm_mix_g, w_in, hgrn_lb_logits, hgrn_onorm_g, attn_qnorm_g, attn_knorm_g,
           attn_rel_bias, w_branch_a, w_branch_b, w_out, norm_ffn_g, w_ffn_gate, w_ffn_up,
           w_ffn_down, norm_ple_g, w_ple_gate, w_ple_proj):
    xt = x.reshape(TOKENS, D_MODEL)
    pt = p.reshape(DEPTH, TOKENS, PLE_DIM)
    w_in, w_a, w_b, w_o, w_g, w_u, w_d, w_pg, w_pp = (
        w.astype(BF16) for w in (w_in, w_branch_a, w_branch_b, w_out, w_ffn_gate, w_ffn_up,
                                 w_ffn_down, w_ple_gate, w_ple_proj))
    for li in range(DEPTH):
        qg2 = jnp.concatenate([attn_qnorm_g[li], attn_qnorm_g[li]]).reshape(1, LANES)
        kg2 = jnp.concatenate([attn_knorm_g[li], attn_knorm_g[li]]).reshape(1, LANES)
        z, logf = _inproj(li, xt, norm_mix_g[li].reshape(1, D_MODEL), w_in, hgrn_lb_logits, qg2, kg2)
        oa = _hgrn(z, logf, hgrn_onorm_g[li].reshape(1, A_DV))
        ob = _attn(z, _band_bias(attn_rel_bias[li]))
        xt = _post(li, xt, oa, ob, z, pt, w_a, w_b, w_o, norm_ffn_g[li].reshape(1, D_MODEL),
                   w_g, w_u, w_d, norm_ple_g[li].reshape(1, D_MODEL), w_pg, w_pp)
    return xt.reshape(BATCH, SEQ, D_MODEL)
```

```python
import functools
import math

import jax
import jax.numpy as jnp
from jax import lax
from jax.experimental import pallas as pl
from jax.experimental.pallas import tpu as pltpu

F32 = jnp.float32
BF16 = jnp.bfloat16

D_MODEL = 1024
BATCH = 8
SEQ = 4096
DEPTH = 4
TOKENS = BATCH * SEQ
CHUNK = 64
PLE_DIM = 256
EPS = 1e-6
A_HEADS = 4
A_DK = 128
A_DV = 128
A_WIDTH = A_HEADS * A_DV
B_HEADS = 8
B_HD = 64
B_WIDTH = B_HEADS * B_HD
B_LEFT_CHUNKS = 8
B_PAD = B_LEFT_CHUNKS * CHUNK
B_BAND = B_PAD + CHUNK
MAX_REL = 128
D_FF = 2816
D_IN = 5632
NEG_BIG = -1e30
LOG2E = math.log2(math.e)
ATTN_Q_SCALE = B_HD ** -0.5 * LOG2E

COL_W = 512
W_QA, W_FA, W_IA, W_GA, W_QB, W_KB, W_VB, W_GATE_A, W_GATE_B = 0, 1, 2, 3, 4, 5, 6, 7, 9
N_W_COLS = D_IN // COL_W
Z_OF_W = {0: 0, 2: 1, 3: 2, 4: 3, 5: 4, 6: 5, 7: 6, 8: 7, 9: 8, 10: 9}
Z_QA, Z_IA, Z_GA, Z_QB, Z_KB, Z_VB, Z_GATE_A, Z_GATE_B = 0, 1, 2, 3, 4, 5, 6, 8
Z_WIDTH = D_IN - COL_W

SUBLANES = 8
LANES = 128
VMEM_LIMIT_BYTES = 56 * 1024 * 1024

INPROJ_TM = 512
HGRN_ROWS = 1024
HGRN_NC = 2
ATTN_ROWS = B_PAD
ATTN_QROWS = 2 * CHUNK
ATTN_WIN = B_PAD + ATTN_QROWS
POST_TM = 512
POST_SLICES = 2
SUB = SUBLANES
N_SUB = CHUNK // SUB
SAFE_CHUNK_LOG2 = 115.0


def _params(*sem):
    return pltpu.CompilerParams(dimension_semantics=sem, vmem_limit_bytes=VMEM_LIMIT_BYTES)


def _resident(shape):
    nd = len(shape)
    return pl.BlockSpec(shape, lambda *_: (0,) * nd, pipeline_mode=pl.Buffered(1))


def _layer_weight(li, shape):
    nd = len(shape)
    return pl.BlockSpec((None,) + tuple(shape), lambda *_: (li,) + (0,) * nd,
                        pipeline_mode=pl.Buffered(1))


def _rmsnorm_bf16(x, g):
    ms = jnp.mean(x * x, axis=-1, keepdims=True)
    return (x * lax.rsqrt(ms + EPS) * g).astype(BF16)


def _sigmoid(x):
    return 1.0 / (1.0 + jnp.exp(-x))


def _pair_head_rms(t, g_pair):
    first = lax.broadcasted_iota(jnp.int32, t.shape, 1) < B_HD
    sq = t * t
    s0 = jnp.sum(jnp.where(first, sq, 0.0), axis=-1, keepdims=True)
    s1 = jnp.sum(jnp.where(first, 0.0, sq), axis=-1, keepdims=True)
    ms = jnp.where(first, s0, s1) * (1.0 / B_HD)
    return t * lax.rsqrt(ms + EPS) * g_pair


def _lower_bound(li, lbl):
    e = jnp.exp(lbl - jnp.max(lbl, axis=0, keepdims=True))
    soft = e / jnp.sum(e, axis=0, keepdims=True)
    lb = jnp.zeros((1, COL_W), F32)
    for r in range(1, li + 1):
        lb = lb + soft[r:r + 1, :]
    return lb


def _log2_forget(z, lb):
    t = jnp.exp(-jnp.abs(z))
    num = jnp.where(z >= 0.0, 1.0 + lb * t, lb + t)
    return jnp.where(num > 0.0, jnp.log2(num), z * LOG2E) - jnp.log2(1.0 + t)


def _inproj_kernel(li, x_ref, g_ref, w_ref, lbl_ref, qg_ref, kg_ref, z_ref, logf_ref):
    h = _rmsnorm_bf16(x_ref[...], g_ref[...])
    lb = _lower_bound(li, lbl_ref[...])
    n_slices = 8
    slice_rows = INPROJ_TM // n_slices

    def heads_normed(acc, g_pair, scale):
        tiles = [_pair_head_rms(acc[:, p * LANES:(p + 1) * LANES], g_pair) * scale
                 for p in range(COL_W // LANES)]
        return jnp.concatenate(tiles, axis=1)

    def finish(cb, acc):
        if cb == W_FA:
            logf_ref[...] = acc
            return
        if cb == W_GA:
            acc = acc * _sigmoid(acc)
        elif cb == W_QB:
            acc = heads_normed(acc, qg_ref[...], ATTN_Q_SCALE)
        elif cb == W_KB:
            acc = heads_normed(acc, kg_ref[...], 1.0)
        zc = Z_OF_W[cb]
        z_ref[:, zc * COL_W:(zc + 1) * COL_W] = acc.astype(BF16)

    def forget_slice(s, after):
        rows = slice(s * slice_rows, (s + 1) * slice_rows)
        bits = pltpu.bitcast(after[0:SUBLANES, :], jnp.uint32)
        zero = lax.shift_right_logical(lax.shift_right_logical(bits, jnp.uint32(16)), jnp.uint32(16))
        lb_tied = lb + zero[0:1, :].astype(F32)
        logf_ref[rows, :] = _log2_forget(logf_ref[rows, :], lb_tied)

    pending = None
    for cb in range(N_W_COLS):
        acc = jnp.dot(h, w_ref[:, cb * COL_W:(cb + 1) * COL_W], preferred_element_type=F32)
        if pending is not None:
            finish(*pending)
        pending = (cb, acc)
        s = cb - (N_W_COLS - n_slices)
        if 0 <= s < n_slices:
            forget_slice(s, acc)
    finish(*pending)


def _inproj(li, x, g, w, lb_logits, qg2, kg2):
    tm = INPROJ_TM
    return pl.pallas_call(
        functools.partial(_inproj_kernel, li),
        grid=(TOKENS // tm,),
        in_specs=[
            pl.BlockSpec((tm, D_MODEL), lambda i: (i, 0)),
            _resident((1, D_MODEL)),
            _layer_weight(li, (D_MODEL, D_IN)),
            _resident((DEPTH, COL_W)),
            _resident((1, LANES)),
            _resident((1, LANES)),
        ],
        out_specs=[
            pl.BlockSpec((tm, Z_WIDTH), lambda i: (i, 0)),
            pl.BlockSpec((tm, COL_W), lambda i: (i, 0)),
        ],
        out_shape=[
            jax.ShapeDtypeStruct((TOKENS, Z_WIDTH), BF16),
            jax.ShapeDtypeStruct((TOKENS, COL_W), F32),
        ],
        compiler_params=_params("arbitrary"),
        name="inproj",
    )(x, g, w, lb_logits, qg2, kg2)


def _split_bf16(v, terms):
    parts = []
    for _ in range(terms - 1):
        p = v.astype(BF16)
        parts.append(p)
        v = v - p.astype(F32)
    parts.append(v.astype(BF16))
    return parts


def _chunk_cumsum(lf, terms):
    nrow = lf.shape[0]
    row = lax.broadcasted_iota(jnp.int32, (nrow, nrow), 0)
    col = lax.broadcasted_iota(jnp.int32, (nrow, nrow), 1)
    shift = int(math.log2(CHUNK))
    same_chunk = lax.shift_right_logical(row, shift) == lax.shift_right_logical(col, shift)
    tril = jnp.where((row >= col) & same_chunk, 1.0, 0.0).astype(BF16)
    out = None
    for part in _split_bf16(lf, terms):
        d = jnp.dot(tril, part, preferred_element_type=F32)
        out = d if out is None else out + d
    return out


class _HgrnGroup:
    def __init__(self, refs, r0):
        self.refs, self.r0 = refs, r0
        self.units = [(h, c) for h in range(A_HEADS) for c in range(HGRN_NC)]

    @staticmethod
    def _hcols(h):
        return slice(h * A_DK, (h + 1) * A_DK)

    @staticmethod
    def _crows(c):
        return slice(c * CHUNK, (c + 1) * CHUNK)

    def decays(self):
        q_ref, lf_ref, i_ref, gs_ref = self.refs[:4]
        rows = pl.ds(self.r0, HGRN_NC * CHUNK)
        lf = lf_ref[rows, :]
        b = _chunk_cumsum(lf, 2)
        k = 1.0 - jnp.exp2(lf)
        qs = q_ref[rows, :].astype(F32) * (A_DK ** -0.5)
        self.q_dec = (qs * jnp.exp2(b)).astype(BF16)
        self.k_grow = (k * jnp.exp2(-b)).astype(BF16)
        self.b_last = [b[(c + 1) * CHUNK - 1:(c + 1) * CHUNK, :] for c in range(HGRN_NC)]
        self.strongest = functools.reduce(jnp.minimum, self.b_last)
        self.k_dec = jnp.concatenate(
            [k[self._crows(c), :] * jnp.exp2(self.b_last[c] - b[self._crows(c), :])
             for c in range(HGRN_NC)], axis=0).astype(BF16)
        self.iv = i_ref[rows, :]
        self.gs = gs_ref[rows, :]

    def pair_products(self):
        hc, cr = self._hcols, self._crows
        self.attn = {(h, c): lax.dot_general(self.q_dec[cr(c), hc(h)], self.k_grow[cr(c), hc(h)],
                                             (((1,), (1,)), ((), ())), preferred_element_type=F32)
                     for h, c in self.units}
        self.upd = {(h, c): lax.dot_general(self.iv[cr(c), hc(h)], self.k_dec[cr(c), hc(h)],
                                            (((0,), (0,)), ((), ())), preferred_element_type=F32)
                    for h, c in self.units}

    def state(self):
        st_ref = self.refs[6]
        hc, cr = self._hcols, self._crows
        self.o_state = {}
        for h in range(A_HEADS):
            st = st_ref[h]
            for c in range(HGRN_NC):
                self.o_state[h, c] = lax.dot_general(
                    self.q_dec[cr(c), hc(h)], st.astype(BF16), (((1,), (1,)), ((), ())),
                    preferred_element_type=F32)
                st = st * jnp.exp2(self.b_last[c][:, hc(h)]) + self.upd[h, c]
            st_ref[h] = st

    def outputs(self):
        og_ref, out_ref = self.refs[4], self.refs[5]
        hc, cr = self._hcols, self._crows
        causal = (lax.broadcasted_iota(jnp.int32, (CHUNK, CHUNK), 0)
                  >= lax.broadcasted_iota(jnp.int32, (CHUNK, CHUNK), 1))
        for h, c in self.units:
            a = jnp.where(causal, self.attn[h, c], 0.0).astype(BF16)
            o = jnp.dot(a, self.iv[cr(c), hc(h)], preferred_element_type=F32) + self.o_state[h, c]
            ms = jnp.mean(o * o, axis=-1, keepdims=True)
            y = o * lax.rsqrt(ms + EPS) * og_ref[...] * self.gs[cr(c), hc(h)].astype(F32)
            out_ref[pl.ds(self.r0 + c * CHUNK, CHUNK), hc(h)] = y.astype(BF16)


def _hgrn_fast(refs):
    groups = [_HgrnGroup(refs, g * HGRN_NC * CHUNK) for g in range(HGRN_ROWS // (HGRN_NC * CHUNK))]
    groups[0].decays()
    groups[0].pair_products()
    for g, grp in enumerate(groups):
        if g + 1 < len(groups):
            groups[g + 1].decays()
        grp.state()
        grp.outputs()
        if g + 1 < len(groups):
            groups[g + 1].pair_products()
    return jnp.min(functools.reduce(jnp.minimum, [grp.strongest for grp in groups]))


def _hgrn_head_chunk_exact(qs, lf, iv, st):
    b = _chunk_cumsum(lf, 3)
    k = 1.0 - jnp.exp2(lf)
    b_last = b[CHUNK - 1:CHUNK, :]

    o = lax.dot_general((qs * jnp.exp2(b)).astype(BF16), st.astype(BF16),
                        (((1,), (1,)), ((), ())), preferred_element_type=F32)

    zero_blk = jnp.zeros((SUB, A_DK), F32)
    c_rows = [None] + [b[SUB * i - 1:SUB * i, :] for i in range(1, N_SUB)]
    cref = jnp.concatenate(
        [zero_blk] + [jnp.broadcast_to(c_rows[i], (SUB, A_DK)) for i in range(1, N_SUB)], axis=0)
    q_all = qs * jnp.exp2(b - cref)
    q_parts, k_parts = [], []
    for i in range(1, N_SUB):
        q_parts.append(jnp.concatenate(
            [zero_blk] * i + [q_all[SUB * i:SUB * (i + 1), :]] + [zero_blk] * (N_SUB - 1 - i), axis=0))
        kd = k[:SUB * i, :] * jnp.exp2(c_rows[i] - b[:SUB * i, :])
        k_parts.append(jnp.concatenate([kd, jnp.zeros((CHUNK - SUB * i, A_DK), F32)], axis=0))
    q_cat = jnp.concatenate(q_parts, axis=1).astype(BF16)
    k_cat = jnp.concatenate(k_parts, axis=1).astype(BF16)
    attn = lax.dot_general(q_cat, k_cat, (((1,), (1,)), ((), ())),
                           preferred_element_type=F32)

    t_loc = lax.broadcasted_iota(jnp.int32, (SUB, CHUNK), 0)
    lane = lax.broadcasted_iota(jnp.int32, (SUB, CHUNK), 1)
    diag_blocks = []
    for i in range(N_SUB):
        qi = qs[SUB * i:SUB * (i + 1), :]
        bi = b[SUB * i:SUB * (i + 1), :]
        blk = jnp.zeros((SUB, CHUNK), F32)
        for jj in range(SUB):
            s = SUB * i + jj
            p = qi * k[s:s + 1, :] * jnp.exp2(bi - b[s:s + 1, :])
            colsum = jnp.sum(p, axis=-1, keepdims=True)
            blk = jnp.where((lane == s) & (t_loc >= jj), colsum, blk)
        diag_blocks.append(blk)
    attn = attn + jnp.concatenate(diag_blocks, axis=0)

    o = o + jnp.dot(attn.astype(BF16), iv, preferred_element_type=F32)

    k_dec = (k * jnp.exp2(b_last - b)).astype(BF16)
    st_new = st * jnp.exp2(b_last) + lax.dot_general(
        iv, k_dec, (((0,), (0,)), ((), ())), preferred_element_type=F32)
    return o, st_new


def _hgrn_kernel(q_ref, lf_ref, i_ref, gs_ref, og_ref, out_ref, st_ref, st0_ref):
    @pl.when(pl.program_id(1) == 0)
    def _():
        st_ref[...] = jnp.zeros_like(st_ref)

    st0_ref[...] = st_ref[...]
    strongest = _hgrn_fast((q_ref, lf_ref, i_ref, gs_ref, og_ref, out_ref, st_ref))
    safe = strongest >= -SAFE_CHUNK_LOG2

    @pl.when(jnp.logical_not(safe))
    def _():
        st_ref[...] = st0_ref[...]

        def chunk_body(c, carry):
            r0 = pl.multiple_of(c * CHUNK, CHUNK)
            rows = pl.ds(r0, CHUNK)
            for h in range(A_HEADS):
                cols = slice(h * A_DK, (h + 1) * A_DK)
                qs = q_ref[rows, cols].astype(F32) * (A_DK ** -0.5)
                o, st_new = _hgrn_head_chunk_exact(qs, lf_ref[rows, cols], i_ref[rows, cols], st_ref[h])
                st_ref[h] = st_new
                ms = jnp.mean(o * o, axis=-1, keepdims=True)
                y = o * lax.rsqrt(ms + EPS) * og_ref[...] * gs_ref[rows, cols].astype(F32)
                out_ref[rows, cols] = y.astype(BF16)
            return carry

        lax.fori_loop(0, HGRN_ROWS // CHUNK, chunk_body, 0)


def _hgrn(z, logf, onorm_g):
    nb = SEQ // HGRN_ROWS

    def zcol(cb):
        return pl.BlockSpec((HGRN_ROWS, COL_W), lambda b, n: (b * nb + n, cb))

    return pl.pallas_call(
        _hgrn_kernel,
        grid=(BATCH, nb),
        in_specs=[
            zcol(Z_QA),
            pl.BlockSpec((HGRN_ROWS, COL_W), lambda b, n: (b * nb + n, 0)),
            zcol(Z_IA),
            zcol(Z_GA),
            _resident((1, A_DV)),
        ],
        out_specs=pl.BlockSpec((HGRN_ROWS, A_WIDTH), lambda b, n: (b * nb + n, 0)),
        out_shape=jax.ShapeDtypeStruct((TOKENS, A_WIDTH), BF16),
        scratch_shapes=[pltpu.VMEM((A_HEADS, A_DV, A_DK), F32),
                        pltpu.VMEM((A_HEADS, A_DV, A_DK), F32)],
        compiler_params=_params("arbitrary", "arbitrary"),
        name="hgrn2",
    )(z, logf, z, z, onorm_g)


def _attn_kernel(q_ref, kp_ref, kc_ref, vp_ref, vc_ref, bias_ref, out_ref):
    n = pl.program_id(1)
    rows = ATTN_ROWS
    first_q = lax.broadcasted_iota(jnp.int32, (ATTN_QROWS, LANES), 1) < B_HD

    def run(first_block):
        units = [(u, p) for u in range(rows // ATTN_QROWS) for p in range(B_HEADS // 2)]

        def qrows(u):
            return slice(u * ATTN_QROWS, (u + 1) * ATTN_QROWS)

        def prev_rows(u):
            return slice(u * ATTN_QROWS, rows)

        def cur_rows(u):
            return slice(0, (u + 1) * ATTN_QROWS)

        def cols(p):
            return slice(p * LANES, (p + 1) * LANES)

        def key_dot(a, b_ref, r, p):
            return lax.dot_general(a, b_ref[r, cols(p)], (((1,), (1,)), ((), ())),
                                   preferred_element_type=F32)

        def score(u, p):
            qp = q_ref[qrows(u), cols(p)]
            zero = jnp.zeros_like(qp)
            qq = jnp.concatenate([jnp.where(first_q, qp, zero), jnp.where(first_q, zero, qp)], axis=0)
            s_cur = key_dot(qq, kc_ref, cur_rows(u), p)
            if first_block:
                return s_cur
            return jnp.concatenate([key_dot(qq, kp_ref, prev_rows(u), p), s_cur], axis=1)

        def softmax(u, p, s):
            n_prev = 0 if first_block else rows - u * ATTN_QROWS
            s = s + bias_ref[p, :, ATTN_WIN - (n_prev + (u + 1) * ATTN_QROWS):]
            m = jnp.max(s, axis=-1, keepdims=True)
            e = jnp.exp2(s - m)
            return e.astype(BF16), jnp.sum(e, axis=-1, keepdims=True)

        def emit(u, p, prob, denom):
            n_prev = 0 if first_block else rows - u * ATTN_QROWS
            o = jnp.dot(prob[:, n_prev:], vc_ref[cur_rows(u), cols(p)], preferred_element_type=F32)
            if not first_block:
                o = o + jnp.dot(prob[:, :n_prev], vp_ref[prev_rows(u), cols(p)],
                                preferred_element_type=F32)
            o = o / denom
            out_ref[qrows(u), cols(p)] = jnp.where(
                first_q, o[:ATTN_QROWS], o[ATTN_QROWS:]).astype(BF16)

        scores, soft = {}, {}
        for step in range(len(units) + 2):
            if step < len(units):
                scores[step] = score(*units[step])
            if 0 <= step - 1 < len(units):
                soft[step - 1] = softmax(*units[step - 1], scores.pop(step - 1))
            if 0 <= step - 2 < len(units):
                emit(*units[step - 2], *soft.pop(step - 2))

    @pl.when(n == 0)
    def _():
        run(True)

    @pl.when(n > 0)
    def _():
        run(False)


def _band_bias(rel_bias):
    rb = rel_bias.astype(F32)
    n_far = (CHUNK - 1) + (B_PAD - MAX_REL) + 1
    far = jnp.broadcast_to(rb[:, 2 * MAX_REL:], (B_HEADS, n_far))
    near = rb[:, MAX_REL - (CHUNK - 1):2 * MAX_REL][:, ::-1]
    tb = jnp.concatenate([far, near], axis=1)
    band = jnp.stack([tb[:, CHUNK - 1 - c:CHUNK - 1 - c + B_BAND] for c in range(CHUNK)], axis=1)
    lo = jnp.pad(band, ((0, 0), (0, 0), (0, CHUNK)), constant_values=NEG_BIG)
    hi = jnp.pad(band, ((0, 0), (0, 0), (CHUNK, 0)), constant_values=NEG_BIG)
    unit = jnp.concatenate([lo, hi], axis=1)
    return (unit * LOG2E).reshape(B_HEADS // 2, 2 * ATTN_QROWS, ATTN_WIN)


def _attn(z, bias):
    rows = ATTN_ROWS
    nb = SEQ // rows

    def cur(cb):
        return pl.BlockSpec((rows, COL_W), lambda b, n: (b * nb + n, cb))

    def prev(cb):
        return pl.BlockSpec((rows, COL_W), lambda b, n: (b * nb + jnp.maximum(n - 1, 0), cb))

    return pl.pallas_call(
        _attn_kernel,
        grid=(BATCH, nb),
        in_specs=[
            cur(Z_QB), prev(Z_KB), cur(Z_KB), prev(Z_VB), cur(Z_VB),
            _resident((B_HEADS // 2, 2 * ATTN_QROWS, ATTN_WIN)),
        ],
        out_specs=pl.BlockSpec((rows, B_WIDTH), lambda b, n: (b * nb + n, 0)),
        out_shape=jax.ShapeDtypeStruct((TOKENS, B_WIDTH), BF16),
        compiler_params=_params("arbitrary", "arbitrary"),
        name="band_attn",
    )(z, z, z, z, z, bias)


def _post_kernel(x_ref, oa_ref, ob_ref, ga0_ref, ga1_ref, gb0_ref, gb1_ref, p_ref,
                 wa_ref, wb_ref, wo_ref, gf_ref, wg_ref, wu_ref, wd_ref, gp_ref, wpg_ref, wpp_ref,
                 out_ref):
    n_slices = POST_SLICES
    rows = [slice(s * (POST_TM // n_slices), (s + 1) * (POST_TM // n_slices)) for s in range(n_slices)]

    def merge(r):
        ga = jnp.concatenate([ga0_ref[r, :], ga1_ref[r, :]], axis=1).astype(F32)
        gb = jnp.concatenate([gb0_ref[r, :], gb1_ref[r, :]], axis=1).astype(F32)
        pa = jnp.dot(oa_ref[r, :], wa_ref[...], preferred_element_type=F32)
        pb = jnp.dot(ob_ref[r, :], wb_ref[...], preferred_element_type=F32)
        merged = (_sigmoid(ga) * pa + _sigmoid(gb) * pb).astype(BF16)
        return x_ref[r, :] + jnp.dot(merged, wo_ref[...], preferred_element_type=F32)

    def ffn(r, x1):
        h = _rmsnorm_bf16(x1, gf_ref[...])
        gate = jnp.dot(h, wg_ref[...], preferred_element_type=F32)
        up = jnp.dot(h, wu_ref[...], preferred_element_type=F32)
        act = (gate * _sigmoid(gate) * up).astype(BF16)
        return x1 + jnp.dot(act, wd_ref[...], preferred_element_type=F32)

    def ple(r, x2):
        emb = jnp.dot(p_ref[r, :].astype(BF16), wpp_ref[...], preferred_element_type=F32)
        h2 = _rmsnorm_bf16(x2, gp_ref[...])
        pgate = _sigmoid(jnp.dot(h2, wpg_ref[...], preferred_element_type=F32))
        out_ref[r, :] = x2 + pgate * emb

    x1 = [merge(r) for r in rows]
    x2 = [ffn(r, v) for r, v in zip(rows, x1)]
    for r, v in zip(rows, x2):
        ple(r, v)


def _post(li, x, oa, ob, z, p, wa, wb, wo, gf, wg, wu, wd, gp, wpg, wpp):
    tm = POST_TM

    def zcol(cb):
        return pl.BlockSpec((tm, COL_W), lambda i: (i, cb))

    return pl.pallas_call(
        _post_kernel,
        grid=(TOKENS // tm,),
        in_specs=[
            pl.BlockSpec((tm, D_MODEL), lambda i: (i, 0)),
            pl.BlockSpec((tm, A_WIDTH), lambda i: (i, 0)),
            pl.BlockSpec((tm, B_WIDTH), lambda i: (i, 0)),
            zcol(Z_GATE_A), zcol(Z_GATE_A + 1), zcol(Z_GATE_B), zcol(Z_GATE_B + 1),
            pl.BlockSpec((None, tm, PLE_DIM), lambda i: (li, i, 0)),
            _layer_weight(li, (A_WIDTH, D_MODEL)),
            _layer_weight(li, (B_WIDTH, D_MODEL)),
            _layer_weight(li, (D_MODEL, D_MODEL)),
            _resident((1, D_MODEL)),
            _layer_weight(li, (D_MODEL, D_FF)),
            _layer_weight(li, (D_MODEL, D_FF)),
            _layer_weight(li, (D_FF, D_MODEL)),
            _resident((1, D_MODEL)),
            _layer_weight(li, (D_MODEL, D_MODEL)),
            _layer_weight(li, (PLE_DIM, D_MODEL)),
        ],
        out_specs=pl.BlockSpec((tm, D_MODEL), lambda i: (i, 0)),
        out_shape=jax.ShapeDtypeStruct((TOKENS, D_MODEL), F32),
        compiler_params=_params("arbitrary"),
        name="post_mix",
    )(x, oa, ob, z, z, z, z, p, wa, wb, wo, gf, wg, wu, wd, gp, wpg, wpp)


def kernel(x, p, norm_mix_g, w_in, hgrn_lb_logits, hgrn_onorm_g, attn_qnorm_g, attn_knorm_g,
           attn_rel_bias, w_branch_a, w_branch_b, w_out, norm_ffn_g, w_ffn_gate, w_ffn_up,
           w_ffn_down, norm_ple_g, w_ple_gate, w_ple_proj):
    xt = x.reshape(TOKENS, D_MODEL)
    pt = p.reshape(DEPTH, TOKENS, PLE_DIM)
    w_in, w_a, w_b, w_o, w_g, w_u, w_d, w_pg, w_pp = (
        w.astype(BF16) for w in (w_in, w_branch_a, w_branch_b, w_out, w_ffn_gate, w_ffn_up,
                                 w_ffn_down, w_ple_gate, w_ple_proj))
    for li in range(DEPTH):
        qg2 = jnp.concatenate([attn_qnorm_g[li], attn_qnorm_g[li]]).reshape(1, LANES)
        kg2 = jnp.concatenate([attn_knorm_g[li], attn_knorm_g[li]]).reshape(1, LANES)
        z, logf = _inproj(li, xt, norm_mix_g[li].reshape(1, D_MODEL), w_in, hgrn_lb_logits, qg2, kg2)
        oa = _hgrn(z, logf, hgrn_onorm_g[li].reshape(1, A_DV))
        ob = _attn(z, _band_bias(attn_rel_bias[li]))
        xt = _post(li, xt, oa, ob, z, pt, w_a, w_b, w_o, norm_ffn_g[li].reshape(1, D_MODEL),
                   w_g, w_u, w_d, norm_ple_g[li].reshape(1, D_MODEL), w_pg, w_pp)
    return xt.reshape(BATCH, SEQ, D_MODEL)
```

```python
import functools
import math

import jax
import jax.numpy as jnp
from jax import lax
from jax.experimental import pallas as pl
from jax.experimental.pallas import tpu as pltpu

F32 = jnp.float32
BF16 = jnp.bfloat16

D_MODEL = 1024
BATCH = 8
SEQ = 4096
DEPTH = 4
TOKENS = BATCH * SEQ
CHUNK = 64
PLE_DIM = 256
EPS = 1e-6
A_HEADS = 4
A_DK = 128
A_DV = 128
A_WIDTH = A_HEADS * A_DV
B_HEADS = 8
B_HD = 64
B_WIDTH = B_HEADS * B_HD
B_LEFT_CHUNKS = 8
B_PAD = B_LEFT_CHUNKS * CHUNK
B_BAND = B_PAD + CHUNK
MAX_REL = 128
D_FF = 2816
D_IN = 5632
NEG_BIG = -1e30
LOG2E = math.log2(math.e)
ATTN_Q_SCALE = B_HD ** -0.5 * LOG2E

COL_W = 512
W_QA, W_FA, W_IA, W_GA, W_QB, W_KB, W_VB, W_GATE_A, W_GATE_B = 0, 1, 2, 3, 4, 5, 6, 7, 9
N_W_COLS = D_IN // COL_W
Z_OF_W = {0: 0, 2: 1, 3: 2, 4: 3, 5: 4, 6: 5, 7: 6, 8: 7, 9: 8, 10: 9}
Z_QA, Z_IA, Z_GA, Z_QB, Z_KB, Z_VB, Z_GATE_A, Z_GATE_B = 0, 1, 2, 3, 4, 5, 6, 8
Z_WIDTH = D_IN - COL_W

SUBLANES = 8
LANES = 128
VMEM_LIMIT_BYTES = 56 * 1024 * 1024

INPROJ_TM = 512
HGRN_ROWS = 2048
HGRN_NC = 2
ATTN_ROWS = B_PAD
ATTN_QROWS = 2 * CHUNK
ATTN_WIN = B_PAD + ATTN_QROWS
POST_TM = 512
POST_SLICES = 2
SUB = SUBLANES
N_SUB = CHUNK // SUB
SAFE_CHUNK_LOG2 = 115.0


def _params(*sem):
    return pltpu.CompilerParams(dimension_semantics=sem, vmem_limit_bytes=VMEM_LIMIT_BYTES)


def _resident(shape):
    nd = len(shape)
    return pl.BlockSpec(shape, lambda *_: (0,) * nd, pipeline_mode=pl.Buffered(1))


def _layer_weight(li, shape):
    nd = len(shape)
    return pl.BlockSpec((None,) + tuple(shape), lambda *_: (li,) + (0,) * nd,
                        pipeline_mode=pl.Buffered(1))


def _rmsnorm_bf16(x, g):
    ms = jnp.mean(x * x, axis=-1, keepdims=True)
    return (x * lax.rsqrt(ms + EPS) * g).astype(BF16)


def _sigmoid(x):
    return 1.0 / (1.0 + jnp.exp(-x))


def _pair_head_rms(t, g_pair):
    first = lax.broadcasted_iota(jnp.int32, t.shape, 1) < B_HD
    sq = t * t
    s0 = jnp.sum(jnp.where(first, sq, 0.0), axis=-1, keepdims=True)
    s1 = jnp.sum(jnp.where(first, 0.0, sq), axis=-1, keepdims=True)
    ms = jnp.where(first, s0, s1) * (1.0 / B_HD)
    return t * lax.rsqrt(ms + EPS) * g_pair


def _lower_bound(li, lbl):
    e = jnp.exp(lbl - jnp.max(lbl, axis=0, keepdims=True))
    soft = e / jnp.sum(e, axis=0, keepdims=True)
    lb = jnp.zeros((1, COL_W), F32)
    for r in range(1, li + 1):
        lb = lb + soft[r:r + 1, :]
    return lb


def _log2_forget(z, lb):
    t = jnp.exp(-jnp.abs(z))
    num = jnp.where(z >= 0.0, 1.0 + lb * t, lb + t)
    return jnp.where(num > 0.0, jnp.log2(num), z * LOG2E) - jnp.log2(1.0 + t)


def _inproj_kernel(li, x_ref, g_ref, w_ref, lbl_ref, qg_ref, kg_ref, z_ref, logf_ref):
    h = _rmsnorm_bf16(x_ref[...], g_ref[...])
    lb = _lower_bound(li, lbl_ref[...])
    n_slices = 8
    slice_rows = INPROJ_TM // n_slices

    def heads_normed(acc, g_pair, scale):
        tiles = [_pair_head_rms(acc[:, p * LANES:(p + 1) * LANES], g_pair) * scale
                 for p in range(COL_W // LANES)]
        return jnp.concatenate(tiles, axis=1)

    def finish(cb, acc):
        if cb == W_FA:
            logf_ref[...] = acc
            return
        if cb == W_GA:
            acc = acc * _sigmoid(acc)
        elif cb == W_QB:
            acc = heads_normed(acc, qg_ref[...], ATTN_Q_SCALE)
        elif cb == W_KB:
            acc = heads_normed(acc, kg_ref[...], 1.0)
        zc = Z_OF_W[cb]
        z_ref[:, zc * COL_W:(zc + 1) * COL_W] = acc.astype(BF16)

    def forget_slice(s, after):
        rows = slice(s * slice_rows, (s + 1) * slice_rows)
        bits = pltpu.bitcast(after[0:SUBLANES, :], jnp.uint32)
        zero = lax.shift_right_logical(lax.shift_right_logical(bits, jnp.uint32(16)), jnp.uint32(16))
        lb_tied = lb + zero[0:1, :].astype(F32)
        logf_ref[rows, :] = _log2_forget(logf_ref[rows, :], lb_tied)

    pending = None
    for cb in range(N_W_COLS):
        acc = jnp.dot(h, w_ref[:, cb * COL_W:(cb + 1) * COL_W], preferred_element_type=F32)
        if pending is not None:
            finish(*pending)
        pending = (cb, acc)
        s = cb - (N_W_COLS - n_slices)
        if 0 <= s < n_slices:
            forget_slice(s, acc)
    finish(*pending)


def _inproj(li, x, g, w, lb_logits, qg2, kg2):
    tm = INPROJ_TM
    return pl.pallas_call(
        functools.partial(_inproj_kernel, li),
        grid=(TOKENS // tm,),
        in_specs=[
            pl.BlockSpec((tm, D_MODEL), lambda i: (i, 0)),
            _resident((1, D_MODEL)),
            _layer_weight(li, (D_MODEL, D_IN)),
            _resident((DEPTH, COL_W)),
            _resident((1, LANES)),
            _resident((1, LANES)),
        ],
        out_specs=[
            pl.BlockSpec((tm, Z_WIDTH), lambda i: (i, 0)),
            pl.BlockSpec((tm, COL_W), lambda i: (i, 0)),
        ],
        out_shape=[
            jax.ShapeDtypeStruct((TOKENS, Z_WIDTH), BF16),
            jax.ShapeDtypeStruct((TOKENS, COL_W), F32),
        ],
        compiler_params=_params("arbitrary"),
        name="inproj",
    )(x, g, w, lb_logits, qg2, kg2)


def _split_bf16(v, terms):
    parts = []
    for _ in range(terms - 1):
        p = v.astype(BF16)
        parts.append(p)
        v = v - p.astype(F32)
    parts.append(v.astype(BF16))
    return parts


def _chunk_cumsum(lf, terms):
    nrow = lf.shape[0]
    row = lax.broadcasted_iota(jnp.int32, (nrow, nrow), 0)
    col = lax.broadcasted_iota(jnp.int32, (nrow, nrow), 1)
    shift = int(math.log2(CHUNK))
    same_chunk = lax.shift_right_logical(row, shift) == lax.shift_right_logical(col, shift)
    tril = jnp.where((row >= col) & same_chunk, 1.0, 0.0).astype(BF16)
    out = None
    for part in _split_bf16(lf, terms):
        d = jnp.dot(tril, part, preferred_element_type=F32)
        out = d if out is None else out + d
    return out


class _HgrnGroup:
    def __init__(self, refs, r0):
        self.refs, self.r0 = refs, r0
        self.units = [(h, c) for h in range(A_HEADS) for c in range(HGRN_NC)]

    @staticmethod
    def _hcols(h):
        return slice(h * A_DK, (h + 1) * A_DK)

    @staticmethod
    def _crows(c):
        return slice(c * CHUNK, (c + 1) * CHUNK)

    def decays(self):
        q_ref, lf_ref, i_ref, gs_ref = self.refs[:4]
        rows = pl.ds(self.r0, HGRN_NC * CHUNK)
        lf = lf_ref[rows, :]
        b = _chunk_cumsum(lf, 2)
        k = 1.0 - jnp.exp2(lf)
        qs = q_ref[rows, :].astype(F32) * (A_DK ** -0.5)
        self.q_dec = (qs * jnp.exp2(b)).astype(BF16)
        self.k_grow = (k * jnp.exp2(-b)).astype(BF16)
        self.b_last = [b[(c + 1) * CHUNK - 1:(c + 1) * CHUNK, :] for c in range(HGRN_NC)]
        self.strongest = functools.reduce(jnp.minimum, self.b_last)
        self.k_dec = jnp.concatenate(
            [k[self._crows(c), :] * jnp.exp2(self.b_last[c] - b[self._crows(c), :])
             for c in range(HGRN_NC)], axis=0).astype(BF16)
        self.iv = i_ref[rows, :]
        self.gs = gs_ref[rows, :]

    def pair_products(self):
        hc, cr = self._hcols, self._crows
        self.attn = {(h, c): lax.dot_general(self.q_dec[cr(c), hc(h)], self.k_grow[cr(c), hc(h)],
                                             (((1,), (1,)), ((), ())), preferred_element_type=F32)
                     for h, c in self.units}
        self.upd = {(h, c): lax.dot_general(self.iv[cr(c), hc(h)], self.k_dec[cr(c), hc(h)],
                                            (((0,), (0,)), ((), ())), preferred_element_type=F32)
                    for h, c in self.units}

    def state(self):
        st_ref = self.refs[6]
        hc, cr = self._hcols, self._crows
        self.o_state = {}
        for h in range(A_HEADS):
            st = st_ref[h]
            for c in range(HGRN_NC):
                self.o_state[h, c] = lax.dot_general(
                    self.q_dec[cr(c), hc(h)], st.astype(BF16), (((1,), (1,)), ((), ())),
                    preferred_element_type=F32)
                st = st * jnp.exp2(self.b_last[c][:, hc(h)]) + self.upd[h, c]
            st_ref[h] = st

    def outputs(self):
        og_ref, out_ref = self.refs[4], self.refs[5]
        hc, cr = self._hcols, self._crows
        causal = (lax.broadcasted_iota(jnp.int32, (CHUNK, CHUNK), 0)
                  >= lax.broadcasted_iota(jnp.int32, (CHUNK, CHUNK), 1))
        for h, c in self.units:
            a = jnp.where(causal, self.attn[h, c], 0.0).astype(BF16)
            o = jnp.dot(a, self.iv[cr(c), hc(h)], preferred_element_type=F32) + self.o_state[h, c]
            ms = jnp.mean(o * o, axis=-1, keepdims=True)
            y = o * lax.rsqrt(ms + EPS) * og_ref[...] * self.gs[cr(c), hc(h)].astype(F32)
            out_ref[pl.ds(self.r0 + c * CHUNK, CHUNK), hc(h)] = y.astype(BF16)


def _hgrn_fast(refs):
    groups = [_HgrnGroup(refs, g * HGRN_NC * CHUNK) for g in range(HGRN_ROWS // (HGRN_NC * CHUNK))]
    groups[0].decays()
    groups[0].pair_products()
    for g, grp in enumerate(groups):
        if g + 1 < len(groups):
            groups[g + 1].decays()
        grp.state()
        grp.outputs()
        if g + 1 < len(groups):
            groups[g + 1].pair_products()
    return jnp.min(functools.reduce(jnp.minimum, [grp.strongest for grp in groups]))


def _hgrn_head_chunk_exact(qs, lf, iv, st):
    b = _chunk_cumsum(lf, 3)
    k = 1.0 - jnp.exp2(lf)
    b_last = b[CHUNK - 1:CHUNK, :]

    o = lax.dot_general((qs * jnp.exp2(b)).astype(BF16), st.astype(BF16),
                        (((1,), (1,)), ((), ())), preferred_element_type=F32)

    zero_blk = jnp.zeros((SUB, A_DK), F32)
    c_rows = [None] + [b[SUB * i - 1:SUB * i, :] for i in range(1, N_SUB)]
    cref = jnp.concatenate(
        [zero_blk] + [jnp.broadcast_to(c_rows[i], (SUB, A_DK)) for i in range(1, N_SUB)], axis=0)
    q_all = qs * jnp.exp2(b - cref)
    q_parts, k_parts = [], []
    for i in range(1, N_SUB):
        q_parts.append(jnp.concatenate(
            [zero_blk] * i + [q_all[SUB * i:SUB * (i + 1), :]] + [zero_blk] * (N_SUB - 1 - i), axis=0))
        kd = k[:SUB * i, :] * jnp.exp2(c_rows[i] - b[:SUB * i, :])
        k_parts.append(jnp.concatenate([kd, jnp.zeros((CHUNK - SUB * i, A_DK), F32)], axis=0))
    q_cat = jnp.concatenate(q_parts, axis=1).astype(BF16)
    k_cat = jnp.concatenate(k_parts, axis=1).astype(BF16)
    attn = lax.dot_general(q_cat, k_cat, (((1,), (1,)), ((), ())),
                           preferred_element_type=F32)

    t_loc = lax.broadcasted_iota(jnp.int32, (SUB, CHUNK), 0)
    lane = lax.broadcasted_iota(jnp.int32, (SUB, CHUNK), 1)
    diag_blocks = []
    for i in range(N_SUB):
        qi = qs[SUB * i:SUB * (i + 1), :]
        bi = b[SUB * i:SUB * (i + 1), :]
        blk = jnp.zeros((SUB, CHUNK), F32)
        for jj in range(SUB):
            s = SUB * i + jj
            p = qi * k[s:s + 1, :] * jnp.exp2(bi - b[s:s + 1, :])
            colsum = jnp.sum(p, axis=-1, keepdims=True)
            blk = jnp.where((lane == s) & (t_loc >= jj), colsum, blk)
        diag_blocks.append(blk)
    attn = attn + jnp.concatenate(diag_blocks, axis=0)

    o = o + jnp.dot(attn.astype(BF16), iv, preferred_element_type=F32)

    k_dec = (k * jnp.exp2(b_last - b)).astype(BF16)
    st_new = st * jnp.exp2(b_last) + lax.dot_general(
        iv, k_dec, (((0,), (0,)), ((), ())), preferred_element_type=F32)
    return o, st_new


def _hgrn_kernel(q_ref, lf_ref, i_ref, gs_ref, og_ref, out_ref, st_ref, st0_ref):
    @pl.when(pl.program_id(1) == 0)
    def _():
        st_ref[...] = jnp.zeros_like(st_ref)

    st0_ref[...] = st_ref[...]
    strongest = _hgrn_fast((q_ref, lf_ref, i_ref, gs_ref, og_ref, out_ref, st_ref))
    safe = strongest >= -SAFE_CHUNK_LOG2

    @pl.when(jnp.logical_not(safe))
    def _():
        st_ref[...] = st0_ref[...]

        def chunk_body(c, carry):
            r0 = pl.multiple_of(c * CHUNK, CHUNK)
            rows = pl.ds(r0, CHUNK)
            for h in range(A_HEADS):
                cols = slice(h * A_DK, (h + 1) * A_DK)
                qs = q_ref[rows, cols].astype(F32) * (A_DK ** -0.5)
                o, st_new = _hgrn_head_chunk_exact(qs, lf_ref[rows, cols], i_ref[rows, cols], st_ref[h])
                st_ref[h] = st_new
                ms = jnp.mean(o * o, axis=-1, keepdims=True)
                y = o * lax.rsqrt(ms + EPS) * og_ref[...] * gs_ref[rows, cols].astype(F32)
                out_ref[rows, cols] = y.astype(BF16)
            return carry

        lax.fori_loop(0, HGRN_ROWS // CHUNK, chunk_body, 0)


def _hgrn(z, logf, onorm_g):
    nb = SEQ // HGRN_ROWS

    def zcol(cb):
        return pl.BlockSpec((HGRN_ROWS, COL_W), lambda b, n: (b * nb + n, cb))

    return pl.pallas_call(
        _hgrn_kernel,
        grid=(BATCH, nb),
        in_specs=[
            zcol(Z_QA),
            pl.BlockSpec((HGRN_ROWS, COL_W), lambda b, n: (b * nb + n, 0)),
            zcol(Z_IA),
            zcol(Z_GA),
            _resident((1, A_DV)),
        ],
        out_specs=pl.BlockSpec((HGRN_ROWS, A_WIDTH), lambda b, n: (b * nb + n, 0)),
        out_shape=jax.ShapeDtypeStruct((TOKENS, A_WIDTH), BF16),
        scratch_shapes=[pltpu.VMEM((A_HEADS, A_DV, A_DK), F32),
                        pltpu.VMEM((A_HEADS, A_DV, A_DK), F32)],
        compiler_params=_params("arbitrary", "arbitrary"),
        name="hgrn2",
    )(z, logf, z, z, onorm_g)


def _attn_kernel(q_ref, kp_ref, kc_ref, vp_ref, vc_ref, bias_ref, out_ref):
    n = pl.program_id(1)
    rows = ATTN_ROWS
    first_q = lax.broadcasted_iota(jnp.int32, (ATTN_QROWS, LANES), 1) < B_HD

    def run(first_block):
        units = [(u, p) for u in range(rows // ATTN_QROWS) for p in range(B_HEADS // 2)]

        def qrows(u):
            return slice(u * ATTN_QROWS, (u + 1) * ATTN_QROWS)

        def prev_rows(u):
            return slice(u * ATTN_QROWS, rows)

        def cur_rows(u):
            return slice(0, (u + 1) * ATTN_QROWS)

        def cols(p):
            return slice(p * LANES, (p + 1) * LANES)

        def key_dot(a, b_ref, r, p):
            return lax.dot_general(a, b_ref[r, cols(p)], (((1,), (1,)), ((), ())),
                                   preferred_element_type=F32)

        def score(u, p):
            qp = q_ref[qrows(u), cols(p)]
            zero = jnp.zeros_like(qp)
            qq = jnp.concatenate([jnp.where(first_q, qp, zero), jnp.where(first_q, zero, qp)], axis=0)
            s_cur = key_dot(qq, kc_ref, cur_rows(u), p)
            if first_block:
                return s_cur
            return jnp.concatenate([key_dot(qq, kp_ref, prev_rows(u), p), s_cur], axis=1)

        def softmax(u, p, s):
            n_prev = 0 if first_block else rows - u * ATTN_QROWS
            s = s + bias_ref[p, :, ATTN_WIN - (n_prev + (u + 1) * ATTN_QROWS):]
            m = jnp.max(s, axis=-1, keepdims=True)
            e = jnp.exp2(s - m)
            return e.astype(BF16), jnp.sum(e, axis=-1, keepdims=True)

        def emit(u, p, prob, denom):
            n_prev = 0 if first_block else rows - u * ATTN_QROWS
            o = jnp.dot(prob[:, n_prev:], vc_ref[cur_rows(u), cols(p)], preferred_element_type=F32)
            if not first_block:
                o = o + jnp.dot(prob[:, :n_prev], vp_ref[prev_rows(u), cols(p)],
                                preferred_element_type=F32)
            o = o / denom
            out_ref[qrows(u), cols(p)] = jnp.where(
                first_q, o[:ATTN_QROWS], o[ATTN_QROWS:]).astype(BF16)

        scores, soft = {}, {}
        for step in range(len(units) + 2):
            if step < len(units):
                scores[step] = score(*units[step])
            if 0 <= step - 1 < len(units):
                soft[step - 1] = softmax(*units[step - 1], scores.pop(step - 1))
            if 0 <= step - 2 < len(units):
                emit(*units[step - 2], *soft.pop(step - 2))

    @pl.when(n == 0)
    def _():
        run(True)

    @pl.when(n > 0)
    def _():
        run(False)


def _band_bias(rel_bias):
    rb = rel_bias.astype(F32)
    n_far = (CHUNK - 1) + (B_PAD - MAX_REL) + 1
    far = jnp.broadcast_to(rb[:, 2 * MAX_REL:], (B_HEADS, n_far))
    near = rb[:, MAX_REL - (CHUNK - 1):2 * MAX_REL][:, ::-1]
    tb = jnp.concatenate([far, near], axis=1)
    band = jnp.stack([tb[:, CHUNK - 1 - c:CHUNK - 1 - c + B_BAND] for c in range(CHUNK)], axis=1)
    lo = jnp.pad(band, ((0, 0), (0, 0), (0, CHUNK)), constant_values=NEG_BIG)
    hi = jnp.pad(band, ((0, 0), (0, 0), (CHUNK, 0)), constant_values=NEG_BIG)
    unit = jnp.concatenate([lo, hi], axis=1)
    return (unit * LOG2E).reshape(B_HEADS // 2, 2 * ATTN_QROWS, ATTN_WIN)


def _attn(z, bias):
    rows = ATTN_ROWS
    nb = SEQ // rows

    def cur(cb):
        return pl.BlockSpec((rows, COL_W), lambda b, n: (b * nb + n, cb))

    def prev(cb):
        return pl.BlockSpec((rows, COL_W), lambda b, n: (b * nb + jnp.maximum(n - 1, 0), cb))

    return pl.pallas_call(
        _attn_kernel,
        grid=(BATCH, nb),
        in_specs=[
            cur(Z_QB), prev(Z_KB), cur(Z_KB), prev(Z_VB), cur(Z_VB),
            _resident((B_HEADS // 2, 2 * ATTN_QROWS, ATTN_WIN)),
        ],
        out_specs=pl.BlockSpec((rows, B_WIDTH), lambda b, n: (b * nb + n, 0)),
        out_shape=jax.ShapeDtypeStruct((TOKENS, B_WIDTH), BF16),
        compiler_params=_params("arbitrary", "arbitrary"),
        name="band_attn",
    )(z, z, z, z, z, bias)


def _post_kernel(x_ref, oa_ref, ob_ref, ga0_ref, ga1_ref, gb0_ref, gb1_ref, p_ref,
                 wa_ref, wb_ref, wo_ref, gf_ref, wg_ref, wu_ref, wd_ref, gp_ref, wpg_ref, wpp_ref,
                 out_ref):
    n_slices = POST_SLICES
    rows = [slice(s * (POST_TM // n_slices), (s + 1) * (POST_TM // n_slices)) for s in range(n_slices)]

    def merge(r):
        ga = jnp.concatenate([ga0_ref[r, :], ga1_ref[r, :]], axis=1).astype(F32)
        gb = jnp.concatenate([gb0_ref[r, :], gb1_ref[r, :]], axis=1).astype(F32)
        pa = jnp.dot(oa_ref[r, :], wa_ref[...], preferred_element_type=F32)
        pb = jnp.dot(ob_ref[r, :], wb_ref[...], preferred_element_type=F32)
        merged = (_sigmoid(ga) * pa + _sigmoid(gb) * pb).astype(BF16)
        return x_ref[r, :] + jnp.dot(merged, wo_ref[...], preferred_element_type=F32)

    def ffn(r, x1):
        h = _rmsnorm_bf16(x1, gf_ref[...])
        gate = jnp.dot(h, wg_ref[...], preferred_element_type=F32)
        up = jnp.dot(h, wu_ref[...], preferred_element_type=F32)
        act = (gate * _sigmoid(gate) * up).astype(BF16)
        return x1 + jnp.dot(act, wd_ref[...], preferred_element_type=F32)

    def ple(r, x2):
        emb = jnp.dot(p_ref[r, :].astype(BF16), wpp_ref[...], preferred_element_type=F32)
        h2 = _rmsnorm_bf16(x2, gp_ref[...])
        pgate = _sigmoid(jnp.dot(h2, wpg_ref[...], preferred_element_type=F32))
        out_ref[r, :] = x2 + pgate * emb

    x1 = [merge(r) for r in rows]
    x2 = [ffn(r, v) for r, v in zip(rows, x1)]
    for r, v in zip(rows, x2):
        ple(r, v)


def _post(li, x, oa, ob, z, p, wa, wb, wo, gf, wg, wu, wd, gp, wpg, wpp):
    tm = POST_TM

    def zcol(cb):
        return pl.BlockSpec((tm, COL_W), lambda i: (i, cb))

    return pl.pallas_call(
        _post_kernel,
        grid=(TOKENS // tm,),
        in_specs=[
            pl.BlockSpec((tm, D_MODEL), lambda i: (i, 0)),
            pl.BlockSpec((tm, A_WIDTH), lambda i: (i, 0)),
            pl.BlockSpec((tm, B_WIDTH), lambda i: (i, 0)),
            zcol(Z_GATE_A), zcol(Z_GATE_A + 1), zcol(Z_GATE_B), zcol(Z_GATE_B + 1),
            pl.BlockSpec((None, tm, PLE_DIM), lambda i: (li, i, 0)),
            _layer_weight(li, (A_WIDTH, D_MODEL)),
            _layer_weight(li, (B_WIDTH, D_MODEL)),
            _layer_weight(li, (D_MODEL, D_MODEL)),
            _resident((1, D_MODEL)),
            _layer_weight(li, (D_MODEL, D_FF)),
            _layer_weight(li, (D_MODEL, D_FF)),
            _layer_weight(li, (D_FF, D_MODEL)),
            _resident((1, D_MODEL)),
            _layer_weight(li, (D_MODEL, D_MODEL)),
            _layer_weight(li, (PLE_DIM, D_MODEL)),
        ],
        out_specs=pl.BlockSpec((tm, D_MODEL), lambda i: (i, 0)),
        out_shape=jax.ShapeDtypeStruct((TOKENS, D_MODEL), F32),
        compiler_params=_params("arbitrary"),
        name="post_mix",
    )(x, oa, ob, z, z, z, z, p, wa, wb, wo, gf, wg, wu, wd, gp, wpg, wpp)


def kernel(x, p, norm_mix_g, w_in, hgrn_lb_logits, hgrn_onorm_g, attn_qnorm_g, attn_knorm_g,
           attn_rel_bias, w_branch_a, w_branch_b, w_out, norm_ffn_g, w_ffn_gate, w_ffn_up,
           w_ffn_down, norm_ple_g, w_ple_gate, w_ple_proj):
    xt = x.reshape(TOKENS, D_MODEL)
    pt = p.reshape(DEPTH, TOKENS, PLE_DIM)
    w_in, w_a, w_b, w_o, w_g, w_u, w_d, w_pg, w_pp = (
        w.astype(BF16) for w in (w_in, w_branch_a, w_branch_b, w_out, w_ffn_gate, w_ffn_up,
                                 w_ffn_down, w_ple_gate, w_ple_proj))
    for li in range(DEPTH):
        qg2 = jnp.concatenate([attn_qnorm_g[li], attn_qnorm_g[li]]).reshape(1, LANES)
        kg2 = jnp.concatenate([attn_knorm_g[li], attn_knorm_g[li]]).reshape(1, LANES)
        z, logf = _inproj(li, xt, norm_mix_g[li].reshape(1, D_MODEL), w_in, hgrn_lb_logits, qg2, kg2)
        oa = _hgrn(z, logf, hgrn_onorm_g[li].reshape(1, A_DV))
        ob = _attn(z, _band_bias(attn_rel_bias[li]))
        xt = _post(li, xt, oa, ob, z, pt, w_a, w_b, w_o, norm_ffn_g[li].reshape(1, D_MODEL),
                   w_g, w_u, w_d, norm_ple_g[li].reshape(1, D_MODEL), w_pg, w_pp)
    return xt.reshape(BATCH, SEQ, D_MODEL)
```

```python
import functools
import math

import jax
import jax.numpy as jnp
from jax import lax
from jax.experimental import pallas as pl
from jax.experimental.pallas import tpu as pltpu

F32 = jnp.float32
BF16 = jnp.bfloat16

D_MODEL = 1024
BATCH = 8
SEQ = 4096
DEPTH = 4
TOKENS = BATCH * SEQ
CHUNK = 64
PLE_DIM = 256
EPS = 1e-6
A_HEADS = 4
A_DK = 128
A_DV = 128
A_WIDTH = A_HEADS * A_DV
B_HEADS = 8
B_HD = 64
B_WIDTH = B_HEADS * B_HD
B_LEFT_CHUNKS = 8
B_PAD = B_LEFT_CHUNKS * CHUNK
B_BAND = B_PAD + CHUNK
MAX_REL = 128
D_FF = 2816
D_IN = 5632
NEG_BIG = -1e30
LOG2E = math.log2(math.e)
ATTN_Q_SCALE = B_HD ** -0.5 * LOG2E

COL_W = 512
W_QA, W_FA, W_IA, W_GA, W_QB, W_KB, W_VB, W_GATE_A, W_GATE_B = 0, 1, 2, 3, 4, 5, 6, 7, 9
N_W_COLS = D_IN // COL_W
Z_OF_W = {0: 4, 2: 5, 3: 6, 4: 7, 5: 8, 6: 9, 7: 0, 8: 1, 9: 2, 10: 3}
Z_QA, Z_IA, Z_GA, Z_QB, Z_KB, Z_VB, Z_GATE_A, Z_GATE_B = 4, 5, 6, 7, 8, 9, 0, 2
Z_WIDTH = D_IN - COL_W

SUBLANES = 8
LANES = 128
VMEM_LIMIT_BYTES = 56 * 1024 * 1024

INPROJ_TM = 512
HGRN_ROWS = 2048
HGRN_NC = 2
ATTN_ROWS = B_PAD
ATTN_QROWS = 2 * CHUNK
ATTN_WIN = B_PAD + ATTN_QROWS
POST_TM = 512
POST_SLICES = 2
SUB = SUBLANES
N_SUB = CHUNK // SUB
SAFE_CHUNK_LOG2 = 115.0


def _params(*sem):
    return pltpu.CompilerParams(dimension_semantics=sem, vmem_limit_bytes=VMEM_LIMIT_BYTES)


def _resident(shape):
    nd = len(shape)
    return pl.BlockSpec(shape, lambda *_: (0,) * nd, pipeline_mode=pl.Buffered(1))


def _layer_weight(li, shape):
    nd = len(shape)
    return pl.BlockSpec((None,) + tuple(shape), lambda *_: (li,) + (0,) * nd,
                        pipeline_mode=pl.Buffered(1))


def _rmsnorm_bf16(x, g):
    ms = jnp.mean(x * x, axis=-1, keepdims=True)
    return (x * lax.rsqrt(ms + EPS) * g).astype(BF16)


def _sigmoid(x):
    return 1.0 / (1.0 + jnp.exp(-x))


def _pair_head_rms(t, g_pair):
    first = lax.broadcasted_iota(jnp.int32, t.shape, 1) < B_HD
    sq = t * t
    s0 = jnp.sum(jnp.where(first, sq, 0.0), axis=-1, keepdims=True)
    s1 = jnp.sum(jnp.where(first, 0.0, sq), axis=-1, keepdims=True)
    ms = jnp.where(first, s0, s1) * (1.0 / B_HD)
    return t * lax.rsqrt(ms + EPS) * g_pair


def _lower_bound(li, lbl):
    e = jnp.exp(lbl - jnp.max(lbl, axis=0, keepdims=True))
    soft = e / jnp.sum(e, axis=0, keepdims=True)
    lb = jnp.zeros((1, COL_W), F32)
    for r in range(1, li + 1):
        lb = lb + soft[r:r + 1, :]
    return lb


def _log2_forget(z, lb):
    t = jnp.exp(-jnp.abs(z))
    num = jnp.where(z >= 0.0, 1.0 + lb * t, lb + t)
    return jnp.where(num > 0.0, jnp.log2(num), z * LOG2E) - jnp.log2(1.0 + t)


def _inproj_kernel(li, x_ref, g_ref, w_ref, lbl_ref, qg_ref, kg_ref, z_ref, logf_ref):
    h = _rmsnorm_bf16(x_ref[...], g_ref[...])
    lb = _lower_bound(li, lbl_ref[...])
    n_slices = 8
    slice_rows = INPROJ_TM // n_slices

    def heads_normed(acc, g_pair, scale):
        tiles = [_pair_head_rms(acc[:, p * LANES:(p + 1) * LANES], g_pair) * scale
                 for p in range(COL_W // LANES)]
        return jnp.concatenate(tiles, axis=1)

    def finish(cb, acc):
        if cb == W_FA:
            logf_ref[...] = acc
            return
        if cb == W_GA:
            acc = acc * _sigmoid(acc)
        elif cb == W_QB:
            acc = heads_normed(acc, qg_ref[...], ATTN_Q_SCALE)
        elif cb == W_KB:
            acc = heads_normed(acc, kg_ref[...], 1.0)
        zc = Z_OF_W[cb]
        z_ref[:, zc * COL_W:(zc + 1) * COL_W] = acc.astype(BF16)

    def forget_slice(s, after):
        rows = slice(s * slice_rows, (s + 1) * slice_rows)
        bits = pltpu.bitcast(after[0:SUBLANES, :], jnp.uint32)
        zero = lax.shift_right_logical(lax.shift_right_logical(bits, jnp.uint32(16)), jnp.uint32(16))
        lb_tied = lb + zero[0:1, :].astype(F32)
        logf_ref[rows, :] = _log2_forget(logf_ref[rows, :], lb_tied)

    pending = None
    for cb in range(N_W_COLS):
        acc = jnp.dot(h, w_ref[:, cb * COL_W:(cb + 1) * COL_W], preferred_element_type=F32)
        if pending is not None:
            finish(*pending)
        pending = (cb, acc)
        s = cb - (N_W_COLS - n_slices)
        if 0 <= s < n_slices:
            forget_slice(s, acc)
    finish(*pending)


def _inproj(li, x, g, w, lb_logits, qg2, kg2):
    tm = INPROJ_TM
    return pl.pallas_call(
        functools.partial(_inproj_kernel, li),
        grid=(TOKENS // tm,),
        in_specs=[
            pl.BlockSpec((tm, D_MODEL), lambda i: (i, 0)),
            _resident((1, D_MODEL)),
            _layer_weight(li, (D_MODEL, D_IN)),
            _resident((DEPTH, COL_W)),
            _resident((1, LANES)),
            _resident((1, LANES)),
        ],
        out_specs=[
            pl.BlockSpec((tm, Z_WIDTH), lambda i: (i, 0)),
            pl.BlockSpec((tm, COL_W), lambda i: (i, 0)),
        ],
        out_shape=[
            jax.ShapeDtypeStruct((TOKENS, Z_WIDTH), BF16),
            jax.ShapeDtypeStruct((TOKENS, COL_W), F32),
        ],
        compiler_params=_params("arbitrary"),
        name="inproj",
    )(x, g, w, lb_logits, qg2, kg2)


def _split_bf16(v, terms):
    parts = []
    for _ in range(terms - 1):
        p = v.astype(BF16)
        parts.append(p)
        v = v - p.astype(F32)
    parts.append(v.astype(BF16))
    return parts


def _chunk_cumsum(lf, terms):
    nrow = lf.shape[0]
    row = lax.broadcasted_iota(jnp.int32, (nrow, nrow), 0)
    col = lax.broadcasted_iota(jnp.int32, (nrow, nrow), 1)
    shift = int(math.log2(CHUNK))
    same_chunk = lax.shift_right_logical(row, shift) == lax.shift_right_logical(col, shift)
    tril = jnp.where((row >= col) & same_chunk, 1.0, 0.0).astype(BF16)
    out = None
    for part in _split_bf16(lf, terms):
        d = jnp.dot(tril, part, preferred_element_type=F32)
        out = d if out is None else out + d
    return out


class _HgrnGroup:
    def __init__(self, refs, r0):
        self.refs, self.r0 = refs, r0
        self.units = [(h, c) for h in range(A_HEADS) for c in range(HGRN_NC)]

    @staticmethod
    def _hcols(h):
        return slice(h * A_DK, (h + 1) * A_DK)

    @staticmethod
    def _crows(c):
        return slice(c * CHUNK, (c + 1) * CHUNK)

    def decays(self):
        q_ref, lf_ref, i_ref, gs_ref = self.refs[:4]
        rows = pl.ds(self.r0, HGRN_NC * CHUNK)
        lf = lf_ref[rows, :]
        b = _chunk_cumsum(lf, 2)
        k = 1.0 - jnp.exp2(lf)
        qs = q_ref[rows, :].astype(F32) * (A_DK ** -0.5)
        self.q_dec = (qs * jnp.exp2(b)).astype(BF16)
        self.k_grow = (k * jnp.exp2(-b)).astype(BF16)
        self.b_last = [b[(c + 1) * CHUNK - 1:(c + 1) * CHUNK, :] for c in range(HGRN_NC)]
        self.strongest = functools.reduce(jnp.minimum, self.b_last)
        self.k_dec = jnp.concatenate(
            [k[self._crows(c), :] * jnp.exp2(self.b_last[c] - b[self._crows(c), :])
             for c in range(HGRN_NC)], axis=0).astype(BF16)
        self.iv = i_ref[rows, :]
        self.gs = gs_ref[rows, :]

    def pair_products(self):
        hc, cr = self._hcols, self._crows
        self.attn = {(h, c): lax.dot_general(self.q_dec[cr(c), hc(h)], self.k_grow[cr(c), hc(h)],
                                             (((1,), (1,)), ((), ())), preferred_element_type=F32)
                     for h, c in self.units}
        self.upd = {(h, c): lax.dot_general(self.iv[cr(c), hc(h)], self.k_dec[cr(c), hc(h)],
                                            (((0,), (0,)), ((), ())), preferred_element_type=F32)
                    for h, c in self.units}

    def state(self):
        st_ref = self.refs[6]
        hc, cr = self._hcols, self._crows
        self.o_state = {}
        for h in range(A_HEADS):
            st = st_ref[h]
            for c in range(HGRN_NC):
                self.o_state[h, c] = lax.dot_general(
                    self.q_dec[cr(c), hc(h)], st.astype(BF16), (((1,), (1,)), ((), ())),
                    preferred_element_type=F32)
                st = st * jnp.exp2(self.b_last[c][:, hc(h)]) + self.upd[h, c]
            st_ref[h] = st

    def outputs(self):
        og_ref, out_ref = self.refs[4], self.refs[5]
        hc, cr = self._hcols, self._crows
        causal = (lax.broadcasted_iota(jnp.int32, (CHUNK, CHUNK), 0)
                  >= lax.broadcasted_iota(jnp.int32, (CHUNK, CHUNK), 1))
        for h, c in self.units:
            a = jnp.where(causal, self.attn[h, c], 0.0).astype(BF16)
            o = jnp.dot(a, self.iv[cr(c), hc(h)], preferred_element_type=F32) + self.o_state[h, c]
            ms = jnp.mean(o * o, axis=-1, keepdims=True)
            y = o * lax.rsqrt(ms + EPS) * og_ref[...] * self.gs[cr(c), hc(h)].astype(F32)
            out_ref[pl.ds(self.r0 + c * CHUNK, CHUNK), hc(h)] = y.astype(BF16)


def _hgrn_fast(refs):
    groups = [_HgrnGroup(refs, g * HGRN_NC * CHUNK) for g in range(HGRN_ROWS // (HGRN_NC * CHUNK))]
    groups[0].decays()
    groups[0].pair_products()
    for g, grp in enumerate(groups):
        if g + 1 < len(groups):
            groups[g + 1].decays()
        grp.state()
        grp.outputs()
        if g + 1 < len(groups):
            groups[g + 1].pair_products()
    return jnp.min(functools.reduce(jnp.minimum, [grp.strongest for grp in groups]))


def _hgrn_head_chunk_exact(qs, lf, iv, st):
    b = _chunk_cumsum(lf, 3)
    k = 1.0 - jnp.exp2(lf)
    b_last = b[CHUNK - 1:CHUNK, :]

    o = lax.dot_general((qs * jnp.exp2(b)).astype(BF16), st.astype(BF16),
                        (((1,), (1,)), ((), ())), preferred_element_type=F32)

    zero_blk = jnp.zeros((SUB, A_DK), F32)
    c_rows = [None] + [b[SUB * i - 1:SUB * i, :] for i in range(1, N_SUB)]
    cref = jnp.concatenate(
        [zero_blk] + [jnp.broadcast_to(c_rows[i], (SUB, A_DK)) for i in range(1, N_SUB)], axis=0)
    q_all = qs * jnp.exp2(b - cref)
    q_parts, k_parts = [], []
    for i in range(1, N_SUB):
        q_parts.append(jnp.concatenate(
            [zero_blk] * i + [q_all[SUB * i:SUB * (i + 1), :]] + [zero_blk] * (N_SUB - 1 - i), axis=0))
        kd = k[:SUB * i, :] * jnp.exp2(c_rows[i] - b[:SUB * i, :])
        k_parts.append(jnp.concatenate([kd, jnp.zeros((CHUNK - SUB * i, A_DK), F32)], axis=0))
    q_cat = jnp.concatenate(q_parts, axis=1).astype(BF16)
    k_cat = jnp.concatenate(k_parts, axis=1).astype(BF16)
    attn = lax.dot_general(q_cat, k_cat, (((1,), (1,)), ((), ())),
                           preferred_element_type=F32)

    t_loc = lax.broadcasted_iota(jnp.int32, (SUB, CHUNK), 0)
    lane = lax.broadcasted_iota(jnp.int32, (SUB, CHUNK), 1)
    diag_blocks = []
    for i in range(N_SUB):
        qi = qs[SUB * i:SUB * (i + 1), :]
        bi = b[SUB * i:SUB * (i + 1), :]
        blk = jnp.zeros((SUB, CHUNK), F32)
        for jj in range(SUB):
            s = SUB * i + jj
            p = qi * k[s:s + 1, :] * jnp.exp2(bi - b[s:s + 1, :])
            colsum = jnp.sum(p, axis=-1, keepdims=True)
            blk = jnp.where((lane == s) & (t_loc >= jj), colsum, blk)
        diag_blocks.append(blk)
    attn = attn + jnp.concatenate(diag_blocks, axis=0)

    o = o + jnp.dot(attn.astype(BF16), iv, preferred_element_type=F32)

    k_dec = (k * jnp.exp2(b_last - b)).astype(BF16)
    st_new = st * jnp.exp2(b_last) + lax.dot_general(
        iv, k_dec, (((0,), (0,)), ((), ())), preferred_element_type=F32)
    return o, st_new


def _hgrn_kernel(q_ref, lf_ref, i_ref, gs_ref, og_ref, out_ref, st_ref, st0_ref):
    @pl.when(pl.program_id(1) == 0)
    def _():
        st_ref[...] = jnp.zeros_like(st_ref)

    st0_ref[...] = st_ref[...]
    strongest = _hgrn_fast((q_ref, lf_ref, i_ref, gs_ref, og_ref, out_ref, st_ref))
    safe = strongest >= -SAFE_CHUNK_LOG2

    @pl.when(jnp.logical_not(safe))
    def _():
        st_ref[...] = st0_ref[...]

        def chunk_body(c, carry):
            r0 = pl.multiple_of(c * CHUNK, CHUNK)
            rows = pl.ds(r0, CHUNK)
            for h in range(A_HEADS):
                cols = slice(h * A_DK, (h + 1) * A_DK)
                qs = q_ref[rows, cols].astype(F32) * (A_DK ** -0.5)
                o, st_new = _hgrn_head_chunk_exact(qs, lf_ref[rows, cols], i_ref[rows, cols], st_ref[h])
                st_ref[h] = st_new
                ms = jnp.mean(o * o, axis=-1, keepdims=True)
                y = o * lax.rsqrt(ms + EPS) * og_ref[...] * gs_ref[rows, cols].astype(F32)
                out_ref[rows, cols] = y.astype(BF16)
            return carry

        lax.fori_loop(0, HGRN_ROWS // CHUNK, chunk_body, 0)


def _hgrn(z, logf, onorm_g):
    nb = SEQ // HGRN_ROWS

    def zcol(cb):
        return pl.BlockSpec((HGRN_ROWS, COL_W), lambda b, n: (b * nb + n, cb))

    return pl.pallas_call(
        _hgrn_kernel,
        grid=(BATCH, nb),
        in_specs=[
            zcol(Z_QA),
            pl.BlockSpec((HGRN_ROWS, COL_W), lambda b, n: (b * nb + n, 0)),
            zcol(Z_IA),
            zcol(Z_GA),
            _resident((1, A_DV)),
        ],
        out_specs=pl.BlockSpec((HGRN_ROWS, A_WIDTH), lambda b, n: (b * nb + n, 0)),
        out_shape=jax.ShapeDtypeStruct((TOKENS, A_WIDTH), BF16),
        scratch_shapes=[pltpu.VMEM((A_HEADS, A_DV, A_DK), F32),
                        pltpu.VMEM((A_HEADS, A_DV, A_DK), F32)],
        compiler_params=_params("arbitrary", "arbitrary"),
        name="hgrn2",
    )(z, logf, z, z, onorm_g)


def _attn_kernel(q_ref, kp_ref, kc_ref, vp_ref, vc_ref, bias_ref, out_ref):
    n = pl.program_id(1)
    rows = ATTN_ROWS
    first_q = lax.broadcasted_iota(jnp.int32, (ATTN_QROWS, LANES), 1) < B_HD

    def run(first_block):
        units = [(u, p) for u in range(rows // ATTN_QROWS) for p in range(B_HEADS // 2)]

        def qrows(u):
            return slice(u * ATTN_QROWS, (u + 1) * ATTN_QROWS)

        def prev_rows(u):
            return slice(u * ATTN_QROWS, rows)

        def cur_rows(u):
            return slice(0, (u + 1) * ATTN_QROWS)

        def cols(p):
            return slice(p * LANES, (p + 1) * LANES)

        def key_dot(a, b_ref, r, p):
            return lax.dot_general(a, b_ref[r, cols(p)], (((1,), (1,)), ((), ())),
                                   preferred_element_type=F32)

        def score(u, p):
            qp = q_ref[qrows(u), cols(p)]
            zero = jnp.zeros_like(qp)
            qq = jnp.concatenate([jnp.where(first_q, qp, zero), jnp.where(first_q, zero, qp)], axis=0)
            s_cur = key_dot(qq, kc_ref, cur_rows(u), p)
            if first_block:
                return s_cur
            return jnp.concatenate([key_dot(qq, kp_ref, prev_rows(u), p), s_cur], axis=1)

        def softmax(u, p, s):
            n_prev = 0 if first_block else rows - u * ATTN_QROWS
            s = s + bias_ref[p, :, ATTN_WIN - (n_prev + (u + 1) * ATTN_QROWS):]
            m = jnp.max(s, axis=-1, keepdims=True)
            e = jnp.exp2(s - m)
            return e.astype(BF16), jnp.sum(e, axis=-1, keepdims=True)

        def emit(u, p, prob, denom):
            n_prev = 0 if first_block else rows - u * ATTN_QROWS
            o = jnp.dot(prob[:, n_prev:], vc_ref[cur_rows(u), cols(p)], preferred_element_type=F32)
            if not first_block:
                o = o + jnp.dot(prob[:, :n_prev], vp_ref[prev_rows(u), cols(p)],
                                preferred_element_type=F32)
            o = o / denom
            out_ref[qrows(u), cols(p)] = jnp.where(
                first_q, o[:ATTN_QROWS], o[ATTN_QROWS:]).astype(BF16)

        scores, soft = {}, {}
        for step in range(len(units) + 2):
            if step < len(units):
                scores[step] = score(*units[step])
            if 0 <= step - 1 < len(units):
                soft[step - 1] = softmax(*units[step - 1], scores.pop(step - 1))
            if 0 <= step - 2 < len(units):
                emit(*units[step - 2], *soft.pop(step - 2))

    @pl.when(n == 0)
    def _():
        run(True)

    @pl.when(n > 0)
    def _():
        run(False)


def _band_bias(rel_bias):
    rb = rel_bias.astype(F32)
    n_far = (CHUNK - 1) + (B_PAD - MAX_REL) + 1
    far = jnp.broadcast_to(rb[:, 2 * MAX_REL:], (B_HEADS, n_far))
    near = rb[:, MAX_REL - (CHUNK - 1):2 * MAX_REL][:, ::-1]
    tb = jnp.concatenate([far, near], axis=1)
    band = jnp.stack([tb[:, CHUNK - 1 - c:CHUNK - 1 - c + B_BAND] for c in range(CHUNK)], axis=1)
    lo = jnp.pad(band, ((0, 0), (0, 0), (0, CHUNK)), constant_values=NEG_BIG)
    hi = jnp.pad(band, ((0, 0), (0, 0), (CHUNK, 0)), constant_values=NEG_BIG)
    unit = jnp.concatenate([lo, hi], axis=1)
    return (unit * LOG2E).reshape(B_HEADS // 2, 2 * ATTN_QROWS, ATTN_WIN)


def _attn(z, bias):
    rows = ATTN_ROWS
    nb = SEQ // rows

    def cur(cb):
        return pl.BlockSpec((rows, COL_W), lambda b, n: (b * nb + n, cb))

    def prev(cb):
        return pl.BlockSpec((rows, COL_W), lambda b, n: (b * nb + jnp.maximum(n - 1, 0), cb))

    return pl.pallas_call(
        _attn_kernel,
        grid=(BATCH, nb),
        in_specs=[
            cur(Z_QB), prev(Z_KB), cur(Z_KB), prev(Z_VB), cur(Z_VB),
            _resident((B_HEADS // 2, 2 * ATTN_QROWS, ATTN_WIN)),
        ],
        out_specs=pl.BlockSpec((rows, B_WIDTH), lambda b, n: (b * nb + n, 0)),
        out_shape=jax.ShapeDtypeStruct((TOKENS, B_WIDTH), BF16),
        compiler_params=_params("arbitrary", "arbitrary"),
        name="band_attn",
    )(z, z, z, z, z, bias)


def _post_kernel(x_ref, oa_ref, ob_ref, gates_ref, p_ref,
                 wa_ref, wb_ref, wo_ref, gf_ref, wg_ref, wu_ref, wd_ref, gp_ref, wpg_ref, wpp_ref,
                 out_ref):
    n_slices = POST_SLICES
    rows = [slice(s * (POST_TM // n_slices), (s + 1) * (POST_TM // n_slices)) for s in range(n_slices)]

    def merge(r):
        ga = gates_ref[r, 0:D_MODEL].astype(F32)
        gb = gates_ref[r, D_MODEL:2 * D_MODEL].astype(F32)
        pa = jnp.dot(oa_ref[r, :], wa_ref[...], preferred_element_type=F32)
        pb = jnp.dot(ob_ref[r, :], wb_ref[...], preferred_element_type=F32)
        merged = (_sigmoid(ga) * pa + _sigmoid(gb) * pb).astype(BF16)
        return x_ref[r, :] + jnp.dot(merged, wo_ref[...], preferred_element_type=F32)

    def ffn(r, x1):
        h = _rmsnorm_bf16(x1, gf_ref[...])
        gate = jnp.dot(h, wg_ref[...], preferred_element_type=F32)
        up = jnp.dot(h, wu_ref[...], preferred_element_type=F32)
        act = (gate * _sigmoid(gate) * up).astype(BF16)
        return x1 + jnp.dot(act, wd_ref[...], preferred_element_type=F32)

    def ple(r, x2):
        emb = jnp.dot(p_ref[r, :].astype(BF16), wpp_ref[...], preferred_element_type=F32)
        h2 = _rmsnorm_bf16(x2, gp_ref[...])
        pgate = _sigmoid(jnp.dot(h2, wpg_ref[...], preferred_element_type=F32))
        out_ref[r, :] = x2 + pgate * emb

    x1 = [merge(r) for r in rows]
    x2 = [ffn(r, v) for r, v in zip(rows, x1)]
    for r, v in zip(rows, x2):
        ple(r, v)


def _post(li, x, oa, ob, z, p, wa, wb, wo, gf, wg, wu, wd, gp, wpg, wpp):
    tm = POST_TM
    return pl.pallas_call(
        _post_kernel,
        grid=(TOKENS // tm,),
        in_specs=[
            pl.BlockSpec((tm, D_MODEL), lambda i: (i, 0)),
            pl.BlockSpec((tm, A_WIDTH), lambda i: (i, 0)),
            pl.BlockSpec((tm, B_WIDTH), lambda i: (i, 0)),
            pl.BlockSpec((tm, 2 * D_MODEL), lambda i: (i, 0)),
            pl.BlockSpec((None, tm, PLE_DIM), lambda i: (li, i, 0)),
            _layer_weight(li, (A_WIDTH, D_MODEL)),
            _layer_weight(li, (B_WIDTH, D_MODEL)),
            _layer_weight(li, (D_MODEL, D_MODEL)),
            _resident((1, D_MODEL)),
            _layer_weight(li, (D_MODEL, D_FF)),
            _layer_weight(li, (D_MODEL, D_FF)),
            _layer_weight(li, (D_FF, D_MODEL)),
            _resident((1, D_MODEL)),
            _layer_weight(li, (D_MODEL, D_MODEL)),
            _layer_weight(li, (PLE_DIM, D_MODEL)),
        ],
        out_specs=pl.BlockSpec((tm, D_MODEL), lambda i: (i, 0)),
        out_shape=jax.ShapeDtypeStruct((TOKENS, D_MODEL), F32),
        compiler_params=_params("arbitrary"),
        name="post_mix",
    )(x, oa, ob, z, p, wa, wb, wo, gf, wg, wu, wd, gp, wpg, wpp)


def kernel(x, p, norm_mix_g, w_in, hgrn_lb_logits, hgrn_onorm_g, attn_qnorm_g, attn_knorm_g,
           attn_rel_bias, w_branch_a, w_branch_b, w_out, norm_ffn_g, w_ffn_gate, w_ffn_up,
           w_ffn_down, norm_ple_g, w_ple_gate, w_ple_proj):
    xt = x.reshape(TOKENS, D_MODEL)
    pt = p.reshape(DEPTH, TOKENS, PLE_DIM)
    w_in, w_a, w_b, w_o, w_g, w_u, w_d, w_pg, w_pp = (
        w.astype(BF16) for w in (w_in, w_branch_a, w_branch_b, w_out, w_ffn_gate, w_ffn_up,
                                 w_ffn_down, w_ple_gate, w_ple_proj))
    for li in range(DEPTH):
        qg2 = jnp.concatenate([attn_qnorm_g[li], attn_qnorm_g[li]]).reshape(1, LANES)
        kg2 = jnp.concatenate([attn_knorm_g[li], attn_knorm_g[li]]).reshape(1, LANES)
        z, logf = _inproj(li, xt, norm_mix_g[li].reshape(1, D_MODEL), w_in, hgrn_lb_logits, qg2, kg2)
        oa = _hgrn(z, logf, hgrn_onorm_g[li].reshape(1, A_DV))
        ob = _attn(z, _band_bias(attn_rel_bias[li]))
        xt = _post(li, xt, oa, ob, z, pt, w_a, w_b, w_o, norm_ffn_g[li].reshape(1, D_MODEL),
                   w_g, w_u, w_d, norm_ple_g[li].reshape(1, D_MODEL), w_pg, w_pp)
    return xt.reshape(BATCH, SEQ, D_MODEL)
```
